```python
import math
import jax, jax.numpy as jnp
from jax import lax
import numpy as np

D_MODEL = 1024
BATCH = 2
SEQ = 8192
DEPTH = 4

CHUNK = 64
N_META = 16
D_MIX = D_MODEL
D_POOL = D_MIX // 2
D_RNN = D_MIX // 2
POOL_WINDOWS = (2, 4, 8, 16)
N_POOL_GROUPS = len(POOL_WINDOWS)
POOL_GROUP_DIM = D_POOL // N_POOL_GROUPS
N_RNN_HEADS = 8
RNN_HEAD_DIM = D_RNN // N_RNN_HEADS
CONV_WIDTH = 4
LRU_C = 8.0
D_IN_PROJ = D_POOL + D_RNN + D_RNN
D_FF = 4 * D_MODEL
EPS = 1e-6

kernel_name = "hybrid_pool_rglru_encoder"


def rms_norm(x, g):
    xf = x.astype(jnp.float32)
    y = xf * lax.rsqrt(jnp.mean(xf * xf, axis=-1, keepdims=True) + EPS)
    return (y * g.astype(jnp.float32)).astype(x.dtype)


def multiscale_pool_mixer(u, pool_w, pool_b, pool_scale):
    B, T, _ = u.shape
    uf = u.astype(jnp.float32)
    cs = jnp.concatenate([jnp.zeros((B, 1, D_POOL), jnp.float32), jnp.cumsum(uf, axis=1)], axis=1)
    upper = cs[:, 1:]
    t_idx = jnp.arange(T, dtype=jnp.float32)[None, :, None]
    pooled = []
    for g, k in enumerate(POOL_WINDOWS):
        sl = slice(g * POOL_GROUP_DIM, (g + 1) * POOL_GROUP_DIM)
        cs_g = cs[:, :, sl]
        lower = jnp.pad(cs_g[:, :T + 1 - k], ((0, 0), (k - 1, 0), (0, 0)))
        count = jnp.minimum(t_idx + 1.0, float(k))
        pooled.append((upper[:, :, sl] - lower) / count)
    pooled = jnp.concatenate(pooled, axis=-1) - uf
    pg = pooled.astype(u.dtype).reshape(B, T, N_POOL_GROUPS, POOL_GROUP_DIM)
    mapped = jnp.einsum('btgi,gij->btgj', pg, pool_w).reshape(B, T, D_POOL) + pool_b
    return mapped * pool_scale


def rglru_mixer(u, gate, conv_w, conv_b, gate_r_w, gate_r_b, gate_i_w, gate_i_b, lru_lambda):
    B, T, _ = u.shape
    upad = jnp.pad(u, ((0, 0), (CONV_WIDTH - 1, 0), (0, 0)))
    xc = conv_b + sum(upad[:, k:k + T] * conv_w[k] for k in range(CONV_WIDTH))
    xh = xc.reshape(B, T, N_RNN_HEADS, RNN_HEAD_DIM)
    r = jax.nn.sigmoid((jnp.einsum('bthi,hij->bthj', xh, gate_r_w).reshape(B, T, D_RNN) + gate_r_b).astype(jnp.float32))
    i = jax.nn.sigmoid((jnp.einsum('bthi,hij->bthj', xh, gate_i_w).reshape(B, T, D_RNN) + gate_i_b).astype(jnp.float32))
    log_a = -LRU_C * r * jax.nn.softplus(-lru_lambda.astype(jnp.float32))
    a = jnp.exp(log_a)
    mult = jnp.sqrt(-jnp.expm1(2.0 * log_a))
    b = mult * (i * xc.astype(jnp.float32))

    def combine(left, right):
        a_l, b_l = left
        a_r, b_r = right
        return a_l * a_r, a_r * b_l + b_r

    _, h = lax.associative_scan(combine, (a, b), axis=1)
    return h.astype(u.dtype) * jax.nn.gelu(gate)


def hybrid_mixer(xn, w_in, pool_w, pool_b, pool_scale, conv_w, conv_b, gate_r_w, gate_r_b,
                 gate_i_w, gate_i_b, lru_lambda, group_norm_g, w_out):
    proj = xn @ w_in
    u_pool = proj[..., :D_POOL]
    u_rnn = proj[..., D_POOL:D_POOL + D_RNN]
    u_gate = proj[..., D_POOL + D_RNN:]
    y_pool = multiscale_pool_mixer(u_pool, pool_w, pool_b, pool_scale)
    y_rnn = rglru_mixer(u_rnn, u_gate, conv_w, conv_b, gate_r_w, gate_r_b, gate_i_w, gate_i_b, lru_lambda)
    y = jnp.concatenate([rms_norm(y_pool, group_norm_g[:D_POOL]),
                         rms_norm(y_rnn, group_norm_g[D_POOL:])], axis=-1)
    return y @ w_out


def sq_relu_mlp(xn, w_up, w_down):
    h = jax.nn.relu(xn @ w_up)
    return (h * h) @ w_down


def setup_inputs(seed: int = 0) -> dict:
    key = jax.random.key(seed)
    ks = jax.random.split(key, 24)
    f32 = jnp.float32
    nrm = lambda k, shape, s: jax.random.normal(k, shape, f32) * s
    a0 = jax.random.uniform(ks[11], (DEPTH, D_RNN), f32, 0.9, 0.999)
    p = a0 ** (1.0 / LRU_C)
    lru_lambda = jnp.log(p) - jnp.log1p(-p)
    return {
        "x": nrm(ks[0], (BATCH, SEQ, D_MODEL), 1.0),
        "meta_tokens": nrm(ks[1], (N_META, D_MODEL), 1.0),
        "mix_norm_g": 1.0 + nrm(ks[2], (DEPTH, D_MODEL), 0.1),
        "w_in": nrm(ks[3], (DEPTH, D_MODEL, D_IN_PROJ), D_MODEL ** -0.5),
        "pool_w": nrm(ks[4], (DEPTH, N_POOL_GROUPS, POOL_GROUP_DIM, POOL_GROUP_DIM), POOL_GROUP_DIM ** -0.5),
        "pool_b": nrm(ks[5], (DEPTH, D_POOL), 0.02),
        "pool_scale": 0.5 + nrm(ks[6], (DEPTH, D_POOL), 0.1),
        "conv_w": nrm(ks[7], (DEPTH, CONV_WIDTH, D_RNN), CONV_WIDTH ** -0.5),
        "conv_b": nrm(ks[8], (DEPTH, D_RNN), 0.02),
        "gate_r_w": nrm(ks[9], (DEPTH, N_RNN_HEADS, RNN_HEAD_DIM, RNN_HEAD_DIM), RNN_HEAD_DIM ** -0.5),
        "gate_r_b": nrm(ks[10], (DEPTH, D_RNN), 0.02),
        "gate_i_w": nrm(ks[12], (DEPTH, N_RNN_HEADS, RNN_HEAD_DIM, RNN_HEAD_DIM), RNN_HEAD_DIM ** -0.5),
        "gate_i_b": nrm(ks[13], (DEPTH, D_RNN), 0.02),
        "lru_lambda": lru_lambda,
        "group_norm_g": 1.0 + nrm(ks[14], (DEPTH, D_MIX), 0.1),
        "w_out": nrm(ks[15], (DEPTH, D_MIX, D_MODEL), D_MIX ** -0.5),
        "mlp_norm_g": 1.0 + nrm(ks[16], (DEPTH, D_MODEL), 0.1),
        "w_up": nrm(ks[17], (DEPTH, D_MODEL, D_FF), D_MODEL ** -0.5),
        "w_down": nrm(ks[18], (DEPTH, D_FF, D_MODEL), D_FF ** -0.5),
        "final_norm_g": 1.0 + nrm(ks[19], (D_MODEL,), 0.1),
    }


def reference(x, meta_tokens, mix_norm_g, w_in, pool_w, pool_b, pool_scale, conv_w, conv_b,
              gate_r_w, gate_r_b, gate_i_w, gate_i_b, lru_lambda, group_norm_g, w_out,
              mlp_norm_g, w_up, w_down, final_norm_g):
    B = x.shape[0]
    meta = jnp.broadcast_to(meta_tokens.astype(x.dtype)[None], (B, N_META, D_MODEL))
    h = jnp.concatenate([meta, x], axis=1)
    for l in range(DEPTH):
        h = h + hybrid_mixer(rms_norm(h, mix_norm_g[l]), w_in[l], pool_w[l], pool_b[l], pool_scale[l],
                             conv_w[l], conv_b[l], gate_r_w[l], gate_r_b[l], gate_i_w[l], gate_i_b[l],
                             lru_lambda[l], group_norm_g[l], w_out[l])
        h = h + sq_relu_mlp(rms_norm(h, mlp_norm_g[l]), w_up[l], w_down[l])
    h = rms_norm(h, final_norm_g)
    return h[:, N_META:]
```

```python
import functools

import jax
import jax.numpy as jnp
from jax import lax
from jax.experimental import pallas as pl
from jax.experimental.pallas import tpu as pltpu

D_MODEL = 1024
N_META = 16
D_POOL = 512
D_RNN = 512
POOL_WINDOWS = (2, 4, 8, 16)
POOL_GROUP_DIM = 128
N_RNN_HEADS = 8
RNN_HEAD_DIM = 64
CONV_WIDTH = 4
LRU_C = 8.0
D_IN_PROJ = D_POOL + 2 * D_RNN
D_FF = 4 * D_MODEL
EPS = 1e-6

SUBLANES = 8
POOL_TAIL = max(POOL_WINDOWS)
CONV_TAIL = SUBLANES
MXU_DIM = 256
FF_CHUNK = 1024
VMEM_LIMIT_BYTES = 56 * 1024 * 1024

F32 = jnp.float32
BF16 = jnp.bfloat16


def _rms(x, g):
    return x * lax.rsqrt(jnp.mean(x * x, axis=-1, keepdims=True) + EPS) * g


def _gelu_tanh(x):
    return 0.5 * x * (1.0 + jnp.tanh(0.7978845608028654 * (x + 0.044715 * (x * x * x))))


def _mixer_kernel(t_offset, tt,
                  h_ref, ptail0_ref, ctail0_ref, hst0_ref,
                  g_ref, w_in_ref, pw_ref, pb_ref, ps_ref, cw_ref, cb_ref, gw_ref, gb_ref,
                  lam_ref, gng_ref, w_out_ref,
                  out_ref, ptail_ref, ctail_ref, hst_ref,
                  pool_buf, rnn_buf, a_buf, b_buf, hn_buf, carry_ref):
    t = pl.program_id(1)

    @pl.when(t == 0)
    def _():
        pool_buf[0:POOL_TAIL, :] = ptail0_ref[...]
        rnn_buf[0:CONV_TAIL, :] = ctail0_ref[...]
        carry_ref[...] = hst0_ref[...]

    h = h_ref[0]
    xn = _rms(h, g_ref[...])
    proj = jnp.dot(xn.astype(BF16), w_in_ref[...], preferred_element_type=F32)
    pool_buf[POOL_TAIL:POOL_TAIL + tt, :] = proj[:, :D_POOL]
    rnn_buf[CONV_TAIL:CONV_TAIL + tt, :] = proj[:, D_POOL:D_POOL + D_RNN]
    gate = proj[:, D_POOL + D_RNN:]

    pooled = []
    for g, k in enumerate(POOL_WINDOWS):
        sl = slice(g * POOL_GROUP_DIM, (g + 1) * POOL_GROUP_DIM)
        u = pool_buf[POOL_TAIL:POOL_TAIL + tt, sl]
        s = u
        for j in range(1, k):
            s = s + pool_buf[POOL_TAIL - j:POOL_TAIL - j + tt, sl]
        if t_offset + 1 >= k:
            inv_count = 1.0 / k
        else:
            frame = t_offset + t * tt + lax.broadcasted_iota(jnp.int32, (tt, 1), 0)
            inv_count = 1.0 / jnp.minimum(frame + 1, k).astype(F32)
        pooled.append(s * inv_count - u)
    mapped = []
    for p in range(2):
        pg = jnp.concatenate(pooled[2 * p:2 * p + 2], axis=-1).astype(BF16)
        mapped.append(jnp.dot(pg, pw_ref[p], preferred_element_type=F32))
    y_pool = (jnp.concatenate(mapped, axis=-1) + pb_ref[...]) * ps_ref[...]
    y_pool = _rms(y_pool, gng_ref[:, :D_POOL])

    xc = cb_ref[...]
    for k in range(CONV_WIDTH):
        off = CONV_TAIL - (CONV_WIDTH - 1) + k
        xc = xc + rnn_buf[off:off + tt, :] * cw_ref[k:k + 1, :]
    xcb = xc.astype(BF16)
    g0 = jnp.dot(xcb[:, :MXU_DIM], gw_ref[0], preferred_element_type=F32)
    g1 = jnp.dot(xcb[:, MXU_DIM:], gw_ref[1], preferred_element_type=F32)
    r = jax.nn.sigmoid(jnp.concatenate([g0[:, :MXU_DIM], g1[:, :MXU_DIM]], axis=-1) + gb_ref[0:1, :])
    i = jax.nn.sigmoid(jnp.concatenate([g0[:, MXU_DIM:], g1[:, MXU_DIM:]], axis=-1) + gb_ref[1:2, :])
    lam = lam_ref[...]
    softplus_neg_lam = jnp.maximum(-lam, 0.0) + jnp.log1p(jnp.exp(-jnp.abs(lam)))
    log_a = (-LRU_C * softplus_neg_lam) * r
    a = jnp.exp(log_a)
    b = jnp.sqrt(jnp.tanh(-log_a) * (1.0 + a * a)) * (i * xc)

    sub = lax.broadcasted_iota(jnp.int32, (tt, D_RNN), 0) & (SUBLANES - 1)
    for d in (1, 2, 4):
        keep = sub >= d
        a_prev = jnp.where(keep, pltpu.roll(a, d, 0), 1.0)
        b_prev = jnp.where(keep, pltpu.roll(b, d, 0), 0.0)
        b = a * b_prev + b
        a = a * a_prev
    a_buf[...] = a
    b_buf[...] = b
    carry = carry_ref[...]
    for rg in range(tt // SUBLANES):
        rows = slice(rg * SUBLANES, (rg + 1) * SUBLANES)
        hg = b_buf[rows, :] + a_buf[rows, :] * carry
        hn_buf[rows, :] = hg
        carry = jnp.broadcast_to(hg[SUBLANES - 1:SUBLANES, :], (SUBLANES, D_RNN))
    carry_ref[...] = carry
    y_rnn = _rms(hn_buf[...] * _gelu_tanh(gate), gng_ref[:, D_POOL:])

    y = jnp.concatenate([y_pool, y_rnn], axis=-1).astype(BF16)
    out_ref[0] = h + jnp.dot(y, w_out_ref[...], preferred_element_type=F32)

    ptail = pool_buf[tt:tt + POOL_TAIL, :]
    ctail = rnn_buf[tt:tt + CONV_TAIL, :]
    pool_buf[0:POOL_TAIL, :] = ptail
    rnn_buf[0:CONV_TAIL, :] = ctail
    ptail_ref[0] = ptail
    ctail_ref[0] = ctail
    hst_ref[0] = carry


def _const_spec(shape):
    return pl.BlockSpec(shape, lambda *_: (0,) * len(shape), pipeline_mode=pl.Buffered(1))


def _mixer_call(h, state, lw, *, t_offset, tt):
    B, T, D = h.shape
    assert T % tt == 0 and tt % SUBLANES == 0 and tt >= POOL_TAIL
    n_t = T // tt
    ptail0, ctail0, hst0 = state
    weights = (lw["mix_g"], lw["w_in"], lw["pool_w"], lw["pool_b"], lw["pool_scale"], lw["conv_w"],
               lw["conv_b"], lw["gate_w"], lw["gate_b"], lw["lam"], lw["gn_g"], lw["w_out"])
    tile = pl.BlockSpec((1, tt, D), lambda b, t: (b, t, 0))
    state_specs = [pl.BlockSpec((1, POOL_TAIL, D_POOL), lambda b, t: (b, 0, 0)),
                   pl.BlockSpec((1, CONV_TAIL, D_RNN), lambda b, t: (b, 0, 0)),
                   pl.BlockSpec((1, SUBLANES, D_RNN), lambda b, t: (b, 0, 0))]
    out, ptail, ctail, hst = pl.pallas_call(
        functools.partial(_mixer_kernel, t_offset, tt),
        grid=(B, n_t),
        in_specs=[tile] + [_const_spec(a.shape) for a in (ptail0, ctail0, hst0) + weights],
        out_specs=[tile] + state_specs,
        out_shape=[jax.ShapeDtypeStruct((B, T, D), F32),
                   jax.ShapeDtypeStruct((B, POOL_TAIL, D_POOL), F32),
                   jax.ShapeDtypeStruct((B, CONV_TAIL, D_RNN), F32),
                   jax.ShapeDtypeStruct((B, SUBLANES, D_RNN), F32)],
        scratch_shapes=[pltpu.VMEM((POOL_TAIL + tt, D_POOL), F32),
                        pltpu.VMEM((CONV_TAIL + tt, D_RNN), F32),
                        pltpu.VMEM((tt, D_RNN), F32),
                        pltpu.VMEM((tt, D_RNN), F32),
                        pltpu.VMEM((tt, D_RNN), F32),
                        pltpu.VMEM((SUBLANES, D_RNN), F32)],
        compiler_params=pltpu.CompilerParams(dimension_semantics=("arbitrary", "arbitrary"),
                                             vmem_limit_bytes=VMEM_LIMIT_BYTES),
        name="mixer",
    )(h, ptail0, ctail0, hst0, *weights)
    return out, (ptail[0], ctail[0], hst[0])


def _mlp_kernel(final_norm, h_ref, g_ref, w_up_ref, w_down_ref, fg_ref, out_ref):
    h = h_ref[...]
    xn = _rms(h, g_ref[...]).astype(BF16)
    acc = h
    for c in range(D_FF // FF_CHUNK):
        cols = slice(c * FF_CHUNK, (c + 1) * FF_CHUNK)
        u = jnp.maximum(jnp.dot(xn, w_up_ref[:, cols], preferred_element_type=F32), 0.0)
        acc = acc + jnp.dot((u * u).astype(BF16), w_down_ref[cols, :], preferred_element_type=F32)
    if final_norm:
        acc = _rms(acc, fg_ref[...])
    out_ref[...] = acc


def _mlp_call(h, lw, final_g, *, final_norm, tt):
    shape = h.shape
    rows = h.reshape(-1, D_MODEL)
    n = rows.shape[0]
    assert n % tt == 0
    tile = pl.BlockSpec((tt, D_MODEL), lambda i: (i, 0))
    weights = (lw["mlp_g"], lw["w_up"], lw["w_down"], final_g)
    out = pl.pallas_call(
        functools.partial(_mlp_kernel, final_norm),
        grid=(n // tt,),
        in_specs=[tile] + [_const_spec(a.shape) for a in weights],
        out_specs=tile,
        out_shape=jax.ShapeDtypeStruct((n, D_MODEL), F32),
        compiler_params=pltpu.CompilerParams(dimension_semantics=("arbitrary",),
                                             vmem_limit_bytes=VMEM_LIMIT_BYTES),
        name="mlp",
    )(rows, *weights)
    return out.reshape(shape)


def _block_diag(w):
    n, d, _ = w.shape
    eye = jnp.eye(n, dtype=w.dtype)
    return (eye[:, None, :, None] * w[:, :, None, :]).reshape(n * d, n * d)


def _layer_weights(l, mix_norm_g, w_in, pool_w, pool_b, pool_scale, conv_w, conv_b, gate_r_w, gate_r_b,
                   gate_i_w, gate_i_b, lru_lambda, group_norm_g, w_out, mlp_norm_g, w_up, w_down):
    row = lambda v: v.reshape(1, -1)
    groups_per_tile = MXU_DIM // POOL_GROUP_DIM
    heads_per_tile = MXU_DIM // RNN_HEAD_DIM
    pw = jnp.stack([_block_diag(pool_w[l, p * groups_per_tile:(p + 1) * groups_per_tile])
                    for p in range(D_POOL // MXU_DIM)])
    gw = jnp.stack([jnp.concatenate(
        [_block_diag(gate_r_w[l, p * heads_per_tile:(p + 1) * heads_per_tile]),
         _block_diag(gate_i_w[l, p * heads_per_tile:(p + 1) * heads_per_tile])], axis=-1)
        for p in range(D_RNN // MXU_DIM)])
    return {
        "mix_g": row(mix_norm_g[l]), "w_in": w_in[l].astype(BF16), "pool_w": pw.astype(BF16),
        "pool_b": row(pool_b[l]), "pool_scale": row(pool_scale[l]), "conv_w": conv_w[l],
        "conv_b": row(conv_b[l]), "gate_w": gw.astype(BF16),
        "gate_b": jnp.stack([gate_r_b[l], gate_i_b[l]]), "lam": row(lru_lambda[l]),
        "gn_g": row(group_norm_g[l]), "w_out": w_out[l].astype(BF16), "mlp_g": row(mlp_norm_g[l]),
        "w_up": w_up[l].astype(BF16), "w_down": w_down[l].astype(BF16),
    }


def kernel(x, meta_tokens, mix_norm_g, w_in, pool_w, pool_b, pool_scale, conv_w, conv_b, gate_r_w, gate_r_b,
           gate_i_w, gate_i_b, lru_lambda, group_norm_g, w_out, mlp_norm_g, w_up, w_down, final_norm_g):
    depth = w_in.shape[0]
    B = x.shape[0]
    lws = [_layer_weights(l, mix_norm_g, w_in, pool_w, pool_b, pool_scale, conv_w, conv_b, gate_r_w,
                          gate_r_b, gate_i_w, gate_i_b, lru_lambda, group_norm_g, w_out, mlp_norm_g,
                          w_up, w_down) for l in range(depth)]
    final_g = final_norm_g.reshape(1, -1)

    zero_state = (jnp.zeros((POOL_TAIL, D_POOL), F32), jnp.zeros((CONV_TAIL, D_RNN), F32),
                  jnp.zeros((SUBLANES, D_RNN), F32))
    hm = meta_tokens.astype(x.dtype)[None]
    states = []
    for l in range(depth):
        hm, st = _mixer_call(hm, zero_state, lws[l], t_offset=0, tt=N_META)
        states.append(st)
        if l + 1 < depth:
            hm = _mlp_call(hm, lws[l], final_g, final_norm=False, tt=N_META)

    h = x
    for l in range(depth):
        h, _ = _mixer_call(h, states[l], lws[l], t_offset=N_META, tt=512)
        h = _mlp_call(h, lws[l], final_g, final_norm=(l + 1 == depth), tt=512)
    return h
```

```python
import functools

import jax
import jax.numpy as jnp
from jax import lax
from jax.experimental import pallas as pl
from jax.experimental.pallas import tpu as pltpu

D_MODEL = 1024
N_META = 16
D_POOL = 512
D_RNN = 512
POOL_WINDOWS = (2, 4, 8, 16)
POOL_GROUP_DIM = 128
N_RNN_HEADS = 8
RNN_HEAD_DIM = 64
CONV_WIDTH = 4
LRU_C = 8.0
D_IN_PROJ = D_POOL + 2 * D_RNN
D_FF = 4 * D_MODEL
EPS = 1e-6

SUBLANES = 8
POOL_TAIL = max(POOL_WINDOWS)
CONV_TAIL = SUBLANES
MXU_DIM = 256
FF_CHUNK = 1024
VMEM_LIMIT_BYTES = 56 * 1024 * 1024

F32 = jnp.float32
BF16 = jnp.bfloat16


def _rms(x, g):
    return x * lax.rsqrt(jnp.mean(x * x, axis=-1, keepdims=True) + EPS) * g


def _sigmoid(x):
    return 0.5 * jnp.tanh(0.5 * x) + 0.5


def _gelu_tanh(x):
    return 0.5 * x * (1.0 + jnp.tanh(0.7978845608028654 * (x + 0.044715 * (x * x * x))))


def _mixer_kernel(t_offset, tt,
                  h_ref, ptail0_ref, ctail0_ref, hst0_ref,
                  g_ref, w_in_ref, pw_ref, pb_ref, ps_ref, cw_ref, cb_ref, gw_ref, rb_ref, ib_ref,
                  lam_ref, gng_ref, w_out_ref,
                  out_ref, ptail_ref, ctail_ref, hst_ref,
                  pool_buf, rnn_buf, a_buf, b_buf, hn_buf, carry_ref, w_in_bf, w_out_bf):
    t = pl.program_id(1)

    @pl.when(t == 0)
    def _():
        w_in_bf[...] = w_in_ref[...].astype(BF16)
        w_out_bf[...] = w_out_ref[...].astype(BF16)
        pool_buf[0:POOL_TAIL, :] = ptail0_ref[...]
        rnn_buf[0:CONV_TAIL, :] = ctail0_ref[...]
        carry_ref[...] = hst0_ref[...]

    h = h_ref[0]
    xn = _rms(h, g_ref[...])
    proj = jnp.dot(xn.astype(BF16), w_in_bf[...], preferred_element_type=F32)
    pool_buf[POOL_TAIL:POOL_TAIL + tt, :] = proj[:, :D_POOL]
    rnn_buf[CONV_TAIL:CONV_TAIL + tt, :] = proj[:, D_POOL:D_POOL + D_RNN]
    gate = proj[:, D_POOL + D_RNN:]

    pooled = []
    for g, k in enumerate(POOL_WINDOWS):
        sl = slice(g * POOL_GROUP_DIM, (g + 1) * POOL_GROUP_DIM)
        u = pool_buf[POOL_TAIL:POOL_TAIL + tt, sl]
        s = u
        for j in range(1, k):
            s = s + pool_buf[POOL_TAIL - j:POOL_TAIL - j + tt, sl]
        if t_offset + 1 >= k:
            inv_count = 1.0 / k
        else:
            frame = t_offset + t * tt + lax.broadcasted_iota(jnp.int32, (tt, 1), 0)
            inv_count = 1.0 / jnp.minimum(frame + 1, k).astype(F32)
        pooled.append(s * inv_count - u)
    mapped = []
    for p in range(D_POOL // MXU_DIM):
        pg = jnp.concatenate(pooled[2 * p:2 * p + 2], axis=-1).astype(BF16)
        mapped.append(jnp.dot(pg, pw_ref[p], preferred_element_type=F32))
    y_pool = (jnp.concatenate(mapped, axis=-1) + pb_ref[...]) * ps_ref[...]
    y_pool = _rms(y_pool, gng_ref[:, :D_POOL])

    xc = cb_ref[...]
    for k in range(CONV_WIDTH):
        off = CONV_TAIL - (CONV_WIDTH - 1) + k
        xc = xc + rnn_buf[off:off + tt, :] * cw_ref[k:k + 1, :]
    xcb = xc.astype(BF16)
    g0 = jnp.dot(xcb[:, :MXU_DIM], gw_ref[0], preferred_element_type=F32)
    g1 = jnp.dot(xcb[:, MXU_DIM:], gw_ref[1], preferred_element_type=F32)
    r = _sigmoid(jnp.concatenate([g0[:, :MXU_DIM], g1[:, :MXU_DIM]], axis=-1) + rb_ref[...])
    i = _sigmoid(jnp.concatenate([g0[:, MXU_DIM:], g1[:, MXU_DIM:]], axis=-1) + ib_ref[...])
    lam = lam_ref[...]
    softplus_neg_lam = jnp.maximum(-lam, 0.0) + jnp.log1p(jnp.exp(-jnp.abs(lam)))
    log_a = (-LRU_C * softplus_neg_lam) * r
    a = jnp.exp(log_a)
    b = jnp.sqrt(jnp.tanh(-log_a) * (1.0 + a * a)) * (i * xc)

    sub = lax.broadcasted_iota(jnp.int32, (tt, D_RNN), 0) & (SUBLANES - 1)
    for d in (1, 2, 4):
        keep = sub >= d
        a_prev = jnp.where(keep, pltpu.roll(a, d, 0), 1.0)
        b_prev = jnp.where(keep, pltpu.roll(b, d, 0), 0.0)
        b = a * b_prev + b
        a = a * a_prev
    a_buf[...] = a
    b_buf[...] = b
    carry = carry_ref[...]
    for rg in range(tt // SUBLANES):
        rows = slice(rg * SUBLANES, (rg + 1) * SUBLANES)
        hg = b_buf[rows, :] + a_buf[rows, :] * carry
        hn_buf[rows, :] = hg
        carry = jnp.broadcast_to(hg[SUBLANES - 1:SUBLANES, :], (SUBLANES, D_RNN))
    carry_ref[...] = carry
    y_rnn = _rms(hn_buf[...] * _gelu_tanh(gate), gng_ref[:, D_POOL:])

    y = jnp.concatenate([y_pool, y_rnn], axis=-1).astype(BF16)
    out_ref[0] = h + jnp.dot(y, w_out_bf[...], preferred_element_type=F32)

    ptail = pool_buf[tt:tt + POOL_TAIL, :]
    ctail = rnn_buf[tt:tt + CONV_TAIL, :]
    pool_buf[0:POOL_TAIL, :] = ptail
    rnn_buf[0:CONV_TAIL, :] = ctail
    ptail_ref[0] = ptail
    ctail_ref[0] = ctail
    hst_ref[0] = carry


def _const_spec(shape):
    return pl.BlockSpec(shape, lambda *_: (0,) * len(shape), pipeline_mode=pl.Buffered(1))


def _layer_spec(arr, l):
    rest = arr.shape[1:]
    return pl.BlockSpec((None,) + rest, lambda *_: (l,) + (0,) * len(rest), pipeline_mode=pl.Buffered(1))


def _mixer_call(h, state, params, l, *, t_offset, tt):
    B, T, D = h.shape
    assert T % tt == 0 and tt % SUBLANES == 0 and tt >= POOL_TAIL
    n_t = T // tt
    weights = tuple(params[k] for k in ("mix_g", "w_in", "pool_w", "pool_b", "pool_scale", "conv_w", "conv_b",
                                        "gate_w", "gate_r_b", "gate_i_b", "lam", "gn_g", "w_out"))
    tile = pl.BlockSpec((1, tt, D), lambda b, t: (b, t, 0))
    state_specs = [pl.BlockSpec((1, POOL_TAIL, D_POOL), lambda b, t: (b, 0, 0)),
                   pl.BlockSpec((1, CONV_TAIL, D_RNN), lambda b, t: (b, 0, 0)),
                   pl.BlockSpec((1, SUBLANES, D_RNN), lambda b, t: (b, 0, 0))]
    out, ptail, ctail, hst = pl.pallas_call(
        functools.partial(_mixer_kernel, t_offset, tt),
        grid=(B, n_t),
        in_specs=[tile] + [_const_spec(a.shape) for a in state] + [_layer_spec(a, l) for a in weights],
        out_specs=[tile] + state_specs,
        out_shape=[jax.ShapeDtypeStruct((B, T, D), F32),
                   jax.ShapeDtypeStruct((B, POOL_TAIL, D_POOL), F32),
                   jax.ShapeDtypeStruct((B, CONV_TAIL, D_RNN), F32),
                   jax.ShapeDtypeStruct((B, SUBLANES, D_RNN), F32)],
        scratch_shapes=[pltpu.VMEM((POOL_TAIL + tt, D_POOL), F32),
                        pltpu.VMEM((CONV_TAIL + tt, D_RNN), F32),
                        pltpu.VMEM((tt, D_RNN), F32),
                        pltpu.VMEM((tt, D_RNN), F32),
                        pltpu.VMEM((tt, D_RNN), F32),
                        pltpu.VMEM((SUBLANES, D_RNN), F32),
                        pltpu.VMEM((D_MODEL, D_IN_PROJ), BF16),
                        pltpu.VMEM((D_MODEL, D_MODEL), BF16)],
        compiler_params=pltpu.CompilerParams(dimension_semantics=("arbitrary", "arbitrary"),
                                             vmem_limit_bytes=VMEM_LIMIT_BYTES),
        name="mixer",
    )(h, *state, *weights)
    return out, (ptail[0], ctail[0], hst[0])


def _mlp_kernel(final_norm, h_ref, g_ref, w_up_ref, w_down_ref, fg_ref, out_ref):
    h = h_ref[...]
    xn = _rms(h, g_ref[...]).astype(BF16)
    acc = h
    for c in range(D_FF // FF_CHUNK):
        cols = slice(c * FF_CHUNK, (c + 1) * FF_CHUNK)
        u = jnp.dot(xn, w_up_ref[:, cols].astype(BF16), preferred_element_type=F32)
        u = jnp.maximum(u, 0.0)
        acc = acc + jnp.dot((u * u).astype(BF16), w_down_ref[cols, :].astype(BF16), preferred_element_type=F32)
    if final_norm:
        acc = _rms(acc, fg_ref[...])
    out_ref[...] = acc


def _mlp_call(h, params, l, *, final_norm, tt):
    shape = h.shape
    rows = h.reshape(-1, D_MODEL)
    n = rows.shape[0]
    assert n % tt == 0
    tile = pl.BlockSpec((tt, D_MODEL), lambda i: (i, 0))
    out = pl.pallas_call(
        functools.partial(_mlp_kernel, final_norm),
        grid=(n // tt,),
        in_specs=[tile, _layer_spec(params["mlp_g"], l), _layer_spec(params["w_up"], l),
                  _layer_spec(params["w_down"], l), _const_spec(params["final_g"].shape)],
        out_specs=tile,
        out_shape=jax.ShapeDtypeStruct((n, D_MODEL), F32),
        compiler_params=pltpu.CompilerParams(dimension_semantics=("arbitrary",),
                                             vmem_limit_bytes=VMEM_LIMIT_BYTES),
        name="mlp",
    )(rows, params["mlp_g"], params["w_up"], params["w_down"], params["final_g"])
    return out.reshape(shape)


def _block_diag(w, per_tile):
    L, n, d, _ = w.shape
    w = w.reshape(L, n // per_tile, per_tile, d, d)
    eye = jnp.eye(per_tile, dtype=w.dtype)
    bd = w[:, :, :, :, None, :] * eye[None, None, :, None, :, None]
    return bd.reshape(L, n // per_tile, per_tile * d, per_tile * d)


def kernel(x, meta_tokens, mix_norm_g, w_in, pool_w, pool_b, pool_scale, conv_w, conv_b, gate_r_w, gate_r_b,
           gate_i_w, gate_i_b, lru_lambda, group_norm_g, w_out, mlp_norm_g, w_up, w_down, final_norm_g):
    depth = w_in.shape[0]
    rows = lambda v: v.reshape(depth, 1, -1)
    gate_w = jnp.concatenate([_block_diag(gate_r_w, MXU_DIM // RNN_HEAD_DIM),
                              _block_diag(gate_i_w, MXU_DIM // RNN_HEAD_DIM)], axis=-1)
    params = {
        "mix_g": rows(mix_norm_g), "w_in": w_in,
        "pool_w": _block_diag(pool_w, MXU_DIM // POOL_GROUP_DIM).astype(BF16),
        "pool_b": rows(pool_b), "pool_scale": rows(pool_scale), "conv_w": conv_w, "conv_b": rows(conv_b),
        "gate_w": gate_w.astype(BF16), "gate_r_b": rows(gate_r_b), "gate_i_b": rows(gate_i_b),
        "lam": rows(lru_lambda), "gn_g": rows(group_norm_g), "w_out": w_out, "mlp_g": rows(mlp_norm_g),
        "w_up": w_up, "w_down": w_down, "final_g": final_norm_g.reshape(1, -1),
    }

    zero_state = (jnp.zeros((POOL_TAIL, D_POOL), F32), jnp.zeros((CONV_TAIL, D_RNN), F32),
                  jnp.zeros((SUBLANES, D_RNN), F32))
    hm = meta_tokens.astype(x.dtype)[None]
    states = []
    for l in range(depth):
        hm, st = _mixer_call(hm, zero_state, params, l, t_offset=0, tt=N_META)
        states.append(st)
        if l + 1 < depth:
            hm = _mlp_call(hm, params, l, final_norm=False, tt=N_META)

    h = x
    for l in range(depth):
        h, _ = _mixer_call(h, states[l], params, l, t_offset=N_META, tt=512)
        h = _mlp_call(h, params, l, final_norm=(l + 1 == depth), tt=512)
    return h
```

```python
import functools

import jax
import jax.numpy as jnp
from jax import lax
from jax.experimental import pallas as pl
from jax.experimental.pallas import tpu as pltpu

D_MODEL = 1024
N_META = 16
D_POOL = 512
D_RNN = 512
POOL_WINDOWS = (2, 4, 8, 16)
POOL_GROUP_DIM = 128
N_RNN_HEADS = 8
RNN_HEAD_DIM = 64
CONV_WIDTH = 4
LRU_C = 8.0
D_IN_PROJ = D_POOL + 2 * D_RNN
D_FF = 4 * D_MODEL
EPS = 1e-6

SUBLANES = 8
LANES = 128
N_SLABS = D_POOL // LANES
POOL_TAIL = max(POOL_WINDOWS)
CONV_TAIL = SUBLANES
CONV_HEAD = CONV_WIDTH - 1
MXU_DIM = 256
FF_CHUNK = 1024
VMEM_LIMIT_BYTES = 56 * 1024 * 1024

F32 = jnp.float32
BF16 = jnp.bfloat16


def _rms(x, g):
    return x * lax.rsqrt(jnp.mean(x * x, axis=-1, keepdims=True) + EPS) * g


def _sigmoid(x):
    return 0.5 * jnp.tanh(0.5 * x) + 0.5


def _gelu_tanh(x):
    return 0.5 * x * (1.0 + jnp.tanh(0.7978845608028654 * (x + 0.044715 * (x * x * x))))


def _in_proj(h, g_ref, w_in_bf):
    xn = _rms(h, g_ref[...])
    return jnp.dot(xn.astype(BF16), w_in_bf[...], preferred_element_type=F32)


def _pool_out(pooled, pw_ref, pb_ref, ps_ref, gng_ref):
    mapped = []
    for p in range(D_POOL // MXU_DIM):
        pg = jnp.concatenate(pooled[2 * p:2 * p + 2], axis=-1).astype(BF16)
        mapped.append(jnp.dot(pg, pw_ref[p], preferred_element_type=F32))
    y_pool = (jnp.concatenate(mapped, axis=-1) + pb_ref[...]) * ps_ref[...]
    return _rms(y_pool, gng_ref[:, :D_POOL])


def _lru_coeffs(xc, gw_ref, rb_ref, ib_ref, lam_ref):
    xcb = xc.astype(BF16)
    g0 = jnp.dot(xcb[:, :MXU_DIM], gw_ref[0], preferred_element_type=F32)
    g1 = jnp.dot(xcb[:, MXU_DIM:], gw_ref[1], preferred_element_type=F32)
    r = _sigmoid(jnp.concatenate([g0[:, :MXU_DIM], g1[:, :MXU_DIM]], axis=-1) + rb_ref[...])
    i = _sigmoid(jnp.concatenate([g0[:, MXU_DIM:], g1[:, MXU_DIM:]], axis=-1) + ib_ref[...])
    lam = lam_ref[...]
    softplus_neg_lam = jnp.maximum(-lam, 0.0) + jnp.log1p(jnp.exp(-jnp.abs(lam)))
    log_a = (-LRU_C * softplus_neg_lam) * r
    a = jnp.exp(log_a)
    b = jnp.sqrt(jnp.tanh(-log_a) * (1.0 + a * a)) * (i * xc)
    return a, b


def _out_proj(h, y_pool, h_lru, gate, gng_ref, w_out_bf):
    y_rnn = _rms(h_lru * _gelu_tanh(gate), gng_ref[:, D_POOL:])
    y = jnp.concatenate([y_pool, y_rnn], axis=-1).astype(BF16)
    return h + jnp.dot(y, w_out_bf[...], preferred_element_type=F32)


def _prefix_mixer_kernel(tt,
                         h_ref, g_ref, w_in_ref, pw_ref, pb_ref, ps_ref, cw_ref, cb_ref, gw_ref, rb_ref, ib_ref,
                         lam_ref, gng_ref, w_out_ref,
                         out_ref, ptail_ref, ctail_ref, hst_ref,
                         pool_buf, rnn_buf, w_in_bf, w_out_bf):
    w_in_bf[...] = w_in_ref[...].astype(BF16)
    w_out_bf[...] = w_out_ref[...].astype(BF16)
    pool_buf[0:POOL_TAIL, :] = jnp.zeros((POOL_TAIL, D_POOL), F32)
    rnn_buf[0:CONV_TAIL, :] = jnp.zeros((CONV_TAIL, D_RNN), F32)

    h = h_ref[0]
    proj = _in_proj(h, g_ref, w_in_bf)
    pool_buf[POOL_TAIL:POOL_TAIL + tt, :] = proj[:, :D_POOL]
    rnn_buf[CONV_TAIL:CONV_TAIL + tt, :] = proj[:, D_POOL:D_POOL + D_RNN]
    gate = proj[:, D_POOL + D_RNN:]

    frame = lax.broadcasted_iota(jnp.int32, (tt, 1), 0)
    pooled = []
    for g, k in enumerate(POOL_WINDOWS):
        sl = slice(g * POOL_GROUP_DIM, (g + 1) * POOL_GROUP_DIM)
        u = pool_buf[POOL_TAIL:POOL_TAIL + tt, sl]
        s = u
        for j in range(1, k):
            s = s + pool_buf[POOL_TAIL - j:POOL_TAIL - j + tt, sl]
        pooled.append(s * (1.0 / jnp.minimum(frame + 1, k).astype(F32)) - u)
    y_pool = _pool_out(pooled, pw_ref, pb_ref, ps_ref, gng_ref)

    xc = cb_ref[...]
    for k in range(CONV_WIDTH):
        off = CONV_TAIL - CONV_HEAD + k
        xc = xc + rnn_buf[off:off + tt, :] * cw_ref[k:k + 1, :]
    a, b = _lru_coeffs(xc, gw_ref, rb_ref, ib_ref, lam_ref)
    state = jnp.zeros((1, D_RNN), F32)
    h_rows = []
    for j in range(tt):
        state = a[j:j + 1, :] * state + b[j:j + 1, :]
        h_rows.append(state)
    h_lru = jnp.concatenate(h_rows, axis=0)

    out_ref[0] = _out_proj(h, y_pool, h_lru, gate, gng_ref, w_out_bf)
    ptail_ref[...] = pool_buf[tt:tt + POOL_TAIL, :]
    ctail_ref[...] = rnn_buf[tt:tt + CONV_TAIL, :]
    hst_ref[...] = jnp.broadcast_to(state, (SUBLANES, D_RNN))


def _fill_heads(e_ref, tb_ref, n_head, seg, sub):
    for g in range(N_SLABS):
        for v in range(n_head):
            cur = e_ref[g, (seg + v) * SUBLANES:(seg + v + 1) * SUBLANES, :]
            prev = tb_ref[g, v * SUBLANES:(v + 1) * SUBLANES, :]
            e_ref[g, v * SUBLANES:(v + 1) * SUBLANES, :] = pltpu.roll(
                jnp.where(sub == SUBLANES - 1, prev, cur), 1, 0)
        tb_ref[g] = e_ref[g, seg * SUBLANES:(seg + n_head) * SUBLANES, :]


def _natural_rows(sm_ref, seg):
    return jnp.concatenate(
        [jnp.concatenate([sm_ref[g, pl.ds(s, seg, stride=SUBLANES), :] for s in range(SUBLANES)], axis=0)
         for g in range(N_SLABS)], axis=-1)


def _mixer_kernel(tt,
                  h_ref, ptail0_ref, ctail0_ref, hst0_ref,
                  g_ref, w_in_ref, pw_ref, pb_ref, ps_ref, cw_ref, cb_ref, gw_ref, rb_ref, ib_ref,
                  lam_ref, gng_ref, w_out_ref,
                  out_ref,
                  pool_e, rnn_e, pool_tb, rnn_tb, pooled_sm, hl_buf, ac_buf, hs_sm,
                  aend_ref, hend_ref, carry_ref, st_ref, w_in_bf, w_out_bf):
    seg = tt // SUBLANES
    t = pl.program_id(1)

    @pl.when(t == 0)
    def _():
        w_in_bf[...] = w_in_ref[...].astype(BF16)
        w_out_bf[...] = w_out_ref[...].astype(BF16)
        for g in range(N_SLABS):
            lanes = slice(g * LANES, (g + 1) * LANES)
            for v in range(POOL_TAIL):
                pool_tb[g, v * SUBLANES:(v + 1) * SUBLANES, :] = jnp.broadcast_to(
                    ptail0_ref[v:v + 1, lanes], (SUBLANES, LANES))
            for v in range(CONV_HEAD):
                row = CONV_TAIL - CONV_HEAD + v
                rnn_tb[g, v * SUBLANES:(v + 1) * SUBLANES, :] = jnp.broadcast_to(
                    ctail0_ref[row:row + 1, lanes], (SUBLANES, LANES))
        st_ref[...] = hst0_ref[...]

    h = h_ref[0]
    proj = _in_proj(h, g_ref, w_in_bf)
    gate = proj[:, D_POOL + D_RNN:]

    for g in range(N_SLABS):
        for s in range(SUBLANES):
            rows = slice(s * seg, (s + 1) * seg)
            pool_e[g, pl.ds(POOL_TAIL * SUBLANES + s, seg, stride=SUBLANES), :] = (
                proj[rows, g * LANES:(g + 1) * LANES])
            rnn_e[g, pl.ds(CONV_HEAD * SUBLANES + s, seg, stride=SUBLANES), :] = (
                proj[rows, D_POOL + g * LANES:D_POOL + (g + 1) * LANES])
    sub = lax.broadcasted_iota(jnp.int32, (SUBLANES, LANES), 0)
    _fill_heads(pool_e, pool_tb, POOL_TAIL, seg, sub)
    _fill_heads(rnn_e, rnn_tb, CONV_HEAD, seg, sub)

    for g, k in enumerate(POOL_WINDOWS):
        e = pool_e[g]
        w = e
        span = 1
        while span < k:
            w = w[span * SUBLANES:] + w[:-span * SUBLANES]
            span *= 2
        pooled_sm[g] = w[-tt:] * (1.0 / k) - e[POOL_TAIL * SUBLANES:]
    pooled_nat = _natural_rows(pooled_sm, seg)
    y_pool = _pool_out([pooled_nat[:, g * LANES:(g + 1) * LANES] for g in range(N_SLABS)],
                       pw_ref, pb_ref, ps_ref, gng_ref)

    xc = []
    for g in range(N_SLABS):
        lanes = slice(g * LANES, (g + 1) * LANES)
        acc = cb_ref[:, lanes]
        for k in range(CONV_WIDTH):
            acc = acc + rnn_e[g, k * SUBLANES:k * SUBLANES + tt, :] * cw_ref[k:k + 1, lanes]
        xc.append(acc)
    a, b = _lru_coeffs(jnp.concatenate(xc, axis=-1), gw_ref, rb_ref, ib_ref, lam_ref)

    hl = ac = None
    for j in range(seg):
        rows = slice(j * SUBLANES, (j + 1) * SUBLANES)
        hl = b[rows] if j == 0 else a[rows] * hl + b[rows]
        ac = a[rows] if j == 0 else a[rows] * ac
        hl_buf[rows, :] = hl
        ac_buf[rows, :] = ac
    aend_ref[...] = ac
    hend_ref[...] = hl
    c = st_ref[0:1, :]
    for s in range(SUBLANES):
        carry_ref[s:s + 1, :] = c
        c = aend_ref[s:s + 1, :] * c + hend_ref[s:s + 1, :]
    st_ref[0:1, :] = c
    carry = carry_ref[...]
    for j in range(seg):
        rows = slice(j * SUBLANES, (j + 1) * SUBLANES)
        hj = hl_buf[rows, :] + ac_buf[rows, :] * carry
        for g in range(N_SLABS):
            hs_sm[g, rows, :] = hj[:, g * LANES:(g + 1) * LANES]
    h_lru = _natural_rows(hs_sm, seg)

    out_ref[0] = _out_proj(h, y_pool, h_lru, gate, gng_ref, w_out_bf)


def _const_spec(shape):
    return pl.BlockSpec(shape, lambda *_: (0,) * len(shape), pipeline_mode=pl.Buffered(1))


def _layer_spec(arr, l):
    rest = arr.shape[1:]
    return pl.BlockSpec((None,) + rest, lambda *_: (l,) + (0,) * len(rest), pipeline_mode=pl.Buffered(1))


_MIXER_WEIGHTS = ("mix_g", "w_in", "pool_w", "pool_b", "pool_scale", "conv_w", "conv_b",
                  "gate_w", "gate_r_b", "gate_i_b", "lam", "gn_g", "w_out")
_BF16_WEIGHT_SCRATCH = [pltpu.VMEM((D_MODEL, D_IN_PROJ), BF16), pltpu.VMEM((D_MODEL, D_MODEL), BF16)]


def _prefix_mixer_call(h, params, l):
    _, tt, D = h.shape
    weights = tuple(params[k] for k in _MIXER_WEIGHTS)
    state_shapes = ((POOL_TAIL, D_POOL), (CONV_TAIL, D_RNN), (SUBLANES, D_RNN))
    out, *state = pl.pallas_call(
        functools.partial(_prefix_mixer_kernel, tt),
        grid=(1,),
        in_specs=[_const_spec(h.shape)] + [_layer_spec(a, l) for a in weights],
        out_specs=[pl.BlockSpec(s, lambda i, n=len(s): (0,) * n) for s in (h.shape,) + state_shapes],
        out_shape=[jax.ShapeDtypeStruct(h.shape, F32)] + [jax.ShapeDtypeStruct(s, F32) for s in state_shapes],
        scratch_shapes=[pltpu.VMEM((POOL_TAIL + tt, D_POOL), F32),
                        pltpu.VMEM((CONV_TAIL + tt, D_RNN), F32)] + _BF16_WEIGHT_SCRATCH,
        compiler_params=pltpu.CompilerParams(dimension_semantics=("arbitrary",),
                                             vmem_limit_bytes=VMEM_LIMIT_BYTES),
        name="prefix_mixer",
    )(h, *weights)
    return out, tuple(state)


def _mixer_call(h, state, params, l, *, tt):
    B, T, D = h.shape
    seg = tt // SUBLANES
    assert T % tt == 0 and tt % (SUBLANES * SUBLANES) == 0 and seg >= POOL_TAIL
    weights = tuple(params[k] for k in _MIXER_WEIGHTS)
    tile = pl.BlockSpec((1, tt, D), lambda b, t: (b, t, 0))
    slab = lambda rows: pltpu.VMEM((N_SLABS, rows, LANES), F32)
    return pl.pallas_call(
        functools.partial(_mixer_kernel, tt),
        grid=(B, T // tt),
        in_specs=[tile] + [_const_spec(a.shape) for a in state] + [_layer_spec(a, l) for a in weights],
        out_specs=tile,
        out_shape=jax.ShapeDtypeStruct((B, T, D), F32),
        scratch_shapes=[slab((POOL_TAIL + seg) * SUBLANES), slab((CONV_HEAD + seg) * SUBLANES),
                        slab(POOL_TAIL * SUBLANES), slab(CONV_HEAD * SUBLANES),
                        slab(tt), pltpu.VMEM((tt, D_RNN), F32), pltpu.VMEM((tt, D_RNN), F32), slab(tt),
                        pltpu.VMEM((SUBLANES, D_RNN), F32), pltpu.VMEM((SUBLANES, D_RNN), F32),
                        pltpu.VMEM((SUBLANES, D_RNN), F32), pltpu.VMEM((SUBLANES, D_RNN), F32)]
        + _BF16_WEIGHT_SCRATCH,
        compiler_params=pltpu.CompilerParams(dimension_semantics=("arbitrary", "arbitrary"),
                                             vmem_limit_bytes=VMEM_LIMIT_BYTES),
        name="mixer",
    )(h, *state, *weights)


def _mlp_kernel(final_norm, h_ref, g_ref, w_up_ref, w_down_ref, fg_ref, out_ref):
    h = h_ref[...]
    xn = _rms(h, g_ref[...]).astype(BF16)
    acc = h
    for c in range(D_FF // FF_CHUNK):
        cols = slice(c * FF_CHUNK, (c + 1) * FF_CHUNK)
        u = jnp.dot(xn, w_up_ref[:, cols].astype(BF16), preferred_element_type=F32)
        u = jnp.maximum(u, 0.0)
        acc = acc + jnp.dot((u * u).astype(BF16), w_down_ref[cols, :].astype(BF16), preferred_element_type=F32)
    if final_norm:
        acc = _rms(acc, fg_ref[...])
    out_ref[...] = acc


def _mlp_call(h, params, l, *, final_norm, tt):
    shape = h.shape
    rows = h.reshape(-1, D_MODEL)
    n = rows.shape[0]
    assert n % tt == 0
    tile = pl.BlockSpec((tt, D_MODEL), lambda i: (i, 0))
    out = pl.pallas_call(
        functools.partial(_mlp_kernel, final_norm),
        grid=(n // tt,),
        in_specs=[tile, _layer_spec(params["mlp_g"], l), _layer_spec(params["w_up"], l),
                  _layer_spec(params["w_down"], l), _const_spec(params["final_g"].shape)],
        out_specs=tile,
        out_shape=jax.ShapeDtypeStruct((n, D_MODEL), F32),
        compiler_params=pltpu.CompilerParams(dimension_semantics=("arbitrary",),
                                             vmem_limit_bytes=VMEM_LIMIT_BYTES),
        name="mlp",
    )(rows, params["mlp_g"], params["w_up"], params["w_down"], params["final_g"])
    return out.reshape(shape)


def _block_diag(w, per_tile):
    L, n, d, _ = w.shape
    w = w.reshape(L, n // per_tile, per_tile, d, d)
    eye = jnp.eye(per_tile, dtype=w.dtype)
    bd = w[:, :, :, :, None, :] * eye[None, None, :, None, :, None]
    return bd.reshape(L, n // per_tile, per_tile * d, per_tile * d)


def kernel(x, meta_tokens, mix_norm_g, w_in, pool_w, pool_b, pool_scale, conv_w, conv_b, gate_r_w, gate_r_b,
           gate_i_w, gate_i_b, lru_lambda, group_norm_g, w_out, mlp_norm_g, w_up, w_down, final_norm_g):
    depth = w_in.shape[0]
    rows = lambda v: v.reshape(depth, 1, -1)
    gate_w = jnp.concatenate([_block_diag(gate_r_w, MXU_DIM // RNN_HEAD_DIM),
                              _block_diag(gate_i_w, MXU_DIM // RNN_HEAD_DIM)], axis=-1)
    params = {
        "mix_g": rows(mix_norm_g), "w_in": w_in,
        "pool_w": _block_diag(pool_w, MXU_DIM // POOL_GROUP_DIM).astype(BF16),
        "pool_b": rows(pool_b), "pool_scale": rows(pool_scale), "conv_w": conv_w, "conv_b": rows(conv_b),
        "gate_w": gate_w.astype(BF16), "gate_r_b": rows(gate_r_b), "gate_i_b": rows(gate_i_b),
        "lam": rows(lru_lambda), "gn_g": rows(group_norm_g), "w_out": w_out, "mlp_g": rows(mlp_norm_g),
        "w_up": w_up, "w_down": w_down, "final_g": final_norm_g.reshape(1, -1),
    }

    hm = meta_tokens.astype(x.dtype)[None]
    states = []
    for l in range(depth):
        hm, st = _prefix_mixer_call(hm, params, l)
        states.append(st)
        if l + 1 < depth:
            hm = _mlp_call(hm, params, l, final_norm=False, tt=N_META)

    h = x
    for l in range(depth):
        h = _mixer_call(h, states[l], params, l, tt=512)
        h = _mlp_call(h, params, l, final_norm=(l + 1 == depth), tt=512)
    return h
```

```python
import functools

import jax
import jax.numpy as jnp
from jax import lax
from jax.experimental import pallas as pl
from jax.experimental.pallas import tpu as pltpu

D_MODEL = 1024
N_META = 16
D_POOL = 512
D_RNN = 512
POOL_WINDOWS = (2, 4, 8, 16)
POOL_GROUP_DIM = 128
N_RNN_HEADS = 8
RNN_HEAD_DIM = 64
CONV_WIDTH = 4
LRU_C = 8.0
D_IN_PROJ = D_POOL + 2 * D_RNN
D_FF = 4 * D_MODEL
EPS = 1e-6

SUBLANES = 8
LANES = 128
N_SLABS = D_POOL // LANES
POOL_TAIL = max(POOL_WINDOWS)
CONV_TAIL = SUBLANES
CONV_HEAD = CONV_WIDTH - 1
MXU_DIM = 256
FF_CHUNK = 1024
VMEM_LIMIT_BYTES = 56 * 1024 * 1024

F32 = jnp.float32
BF16 = jnp.bfloat16


def _rms(x, g):
    return x * lax.rsqrt(jnp.mean(x * x, axis=-1, keepdims=True) + EPS) * g


def _sigmoid(x):
    return 0.5 * jnp.tanh(0.5 * x) + 0.5


def _gelu_tanh(x):
    return 0.5 * x * (1.0 + jnp.tanh(0.7978845608028654 * (x + 0.044715 * (x * x * x))))


def _in_proj(h, g_ref, w_in_bf):
    xn = _rms(h, g_ref[...])
    return jnp.dot(xn.astype(BF16), w_in_bf[...], preferred_element_type=F32)


def _pool_out(pooled, pw_ref, pb_ref, ps_ref, gng_ref):
    mapped = []
    for p in range(D_POOL // MXU_DIM):
        pg = jnp.concatenate(pooled[2 * p:2 * p + 2], axis=-1).astype(BF16)
        mapped.append(jnp.dot(pg, pw_ref[p], preferred_element_type=F32))
    y_pool = (jnp.concatenate(mapped, axis=-1) + pb_ref[...]) * ps_ref[...]
    return _rms(y_pool, gng_ref[:, :D_POOL])


def _lru_coeffs(xc, gw_ref, rb_ref, ib_ref, lam_ref):
    xcb = xc.astype(BF16)
    g0 = jnp.dot(xcb[:, :MXU_DIM], gw_ref[0], preferred_element_type=F32)
    g1 = jnp.dot(xcb[:, MXU_DIM:], gw_ref[1], preferred_element_type=F32)
    r = _sigmoid(jnp.concatenate([g0[:, :MXU_DIM], g1[:, :MXU_DIM]], axis=-1) + rb_ref[...])
    i = _sigmoid(jnp.concatenate([g0[:, MXU_DIM:], g1[:, MXU_DIM:]], axis=-1) + ib_ref[...])
    lam = lam_ref[...]
    softplus_neg_lam = jnp.maximum(-lam, 0.0) + jnp.log1p(jnp.exp(-jnp.abs(lam)))
    log_a = (-LRU_C * softplus_neg_lam) * r
    a = jnp.exp(log_a)
    b = jnp.sqrt(jnp.tanh(-log_a) * (1.0 + a * a)) * (i * xc)
    return a, b


def _out_proj(h, y_pool, h_lru, gate, gng_ref, w_out_bf):
    y_rnn = _rms(h_lru * _gelu_tanh(gate), gng_ref[:, D_POOL:])
    y = jnp.concatenate([y_pool, y_rnn], axis=-1).astype(BF16)
    return h + jnp.dot(y, w_out_bf[...], preferred_element_type=F32)


def _prefix_mixer_kernel(tt,
                         h_ref, g_ref, w_in_ref, pw_ref, pb_ref, ps_ref, cw_ref, cb_ref, gw_ref, rb_ref, ib_ref,
                         lam_ref, gng_ref, w_out_ref,
                         out_ref, ptail_ref, ctail_ref, hst_ref,
                         pool_buf, rnn_buf, w_in_bf, w_out_bf):
    w_in_bf[...] = w_in_ref[...].astype(BF16)
    w_out_bf[...] = w_out_ref[...].astype(BF16)
    pool_buf[0:POOL_TAIL, :] = jnp.zeros((POOL_TAIL, D_POOL), F32)
    rnn_buf[0:CONV_TAIL, :] = jnp.zeros((CONV_TAIL, D_RNN), F32)

    h = h_ref[0]
    proj = _in_proj(h, g_ref, w_in_bf)
    pool_buf[POOL_TAIL:POOL_TAIL + tt, :] = proj[:, :D_POOL]
    rnn_buf[CONV_TAIL:CONV_TAIL + tt, :] = proj[:, D_POOL:D_POOL + D_RNN]
    gate = proj[:, D_POOL + D_RNN:]

    frame = lax.broadcasted_iota(jnp.int32, (tt, 1), 0)
    pooled = []
    for g, k in enumerate(POOL_WINDOWS):
        sl = slice(g * POOL_GROUP_DIM, (g + 1) * POOL_GROUP_DIM)
        u = pool_buf[POOL_TAIL:POOL_TAIL + tt, sl]
        s = u
        for j in range(1, k):
            s = s + pool_buf[POOL_TAIL - j:POOL_TAIL - j + tt, sl]
        pooled.append(s * (1.0 / jnp.minimum(frame + 1, k).astype(F32)) - u)
    y_pool = _pool_out(pooled, pw_ref, pb_ref, ps_ref, gng_ref)

    xc = cb_ref[...]
    for k in range(CONV_WIDTH):
        off = CONV_TAIL - CONV_HEAD + k
        xc = xc + rnn_buf[off:off + tt, :] * cw_ref[k:k + 1, :]
    a, b = _lru_coeffs(xc, gw_ref, rb_ref, ib_ref, lam_ref)
    state = jnp.zeros((1, D_RNN), F32)
    h_rows = []
    for j in range(tt):
        state = a[j:j + 1, :] * state + b[j:j + 1, :]
        h_rows.append(state)
    h_lru = jnp.concatenate(h_rows, axis=0)

    out_ref[0] = _out_proj(h, y_pool, h_lru, gate, gng_ref, w_out_bf)
    ptail_ref[...] = pool_buf[tt:tt + POOL_TAIL, :]
    ctail_ref[...] = rnn_buf[tt:tt + CONV_TAIL, :]
    hst_ref[...] = jnp.broadcast_to(state, (SUBLANES, D_RNN))


def _fill_heads(e_ref, tb_ref, n_head, seg, sub):
    for g in range(N_SLABS):
        for v in range(n_head):
            cur = e_ref[g, (seg + v) * SUBLANES:(seg + v + 1) * SUBLANES, :]
            prev = tb_ref[g, v * SUBLANES:(v + 1) * SUBLANES, :]
            e_ref[g, v * SUBLANES:(v + 1) * SUBLANES, :] = pltpu.roll(
                jnp.where(sub == SUBLANES - 1, prev, cur), 1, 0)
        tb_ref[g] = e_ref[g, seg * SUBLANES:(seg + n_head) * SUBLANES, :]


def _tile_copies(hbm, buf, sem, b, t, slot, tt, to_vmem):
    seg = tt // SUBLANES
    copies = []
    for s in range(SUBLANES):
        rows = hbm.at[b, pl.ds(t * tt + s * seg, seg), :]
        regs = buf.at[slot, :, s, :]
        copies.append(pltpu.make_async_copy(rows, regs, sem.at[slot]) if to_vmem
                      else pltpu.make_async_copy(regs, rows, sem.at[slot]))
    return copies


def _mixer_kernel(tt,
                  h_hbm, ptail0_ref, ctail0_ref, hst0_ref,
                  g_ref, w_in_ref, pw_ref, pb_ref, ps_ref, cw_ref, cb_ref, gw_ref, rb_ref, ib_ref,
                  lam_ref, gng_ref, w_out_ref,
                  out_hbm,
                  in_buf, out_buf, in_sem, out_sem,
                  pool_e, rnn_e, pool_tb, rnn_tb, hl_buf, ac_buf, hs_buf,
                  aend_ref, hend_ref, carry_ref, st_ref, w_in_bf, w_out_bf):
    seg = tt // SUBLANES
    b = pl.program_id(0)
    t = pl.program_id(1)
    n_t = pl.num_programs(1)
    n = b * n_t + t
    n_steps = pl.num_programs(0) * n_t
    slot = n % 2

    @pl.when(n == 0)
    def _():
        for c in _tile_copies(h_hbm, in_buf, in_sem, b, t, slot, tt, True):
            c.start()

    @pl.when(n + 1 < n_steps)
    def _():
        wrap = t + 1 == n_t
        for c in _tile_copies(h_hbm, in_buf, in_sem, jnp.where(wrap, b + 1, b), jnp.where(wrap, 0, t + 1),
                              1 - slot, tt, True):
            c.start()

    @pl.when(t == 0)
    def _():
        w_in_bf[...] = w_in_ref[...].astype(BF16)
        w_out_bf[...] = w_out_ref[...].astype(BF16)
        for g in range(N_SLABS):
            lanes = slice(g * LANES, (g + 1) * LANES)
            for v in range(POOL_TAIL):
                pool_tb[g, v * SUBLANES:(v + 1) * SUBLANES, :] = jnp.broadcast_to(
                    ptail0_ref[v:v + 1, lanes], (SUBLANES, LANES))
            for v in range(CONV_HEAD):
                row = CONV_TAIL - CONV_HEAD + v
                rnn_tb[g, v * SUBLANES:(v + 1) * SUBLANES, :] = jnp.broadcast_to(
                    ctail0_ref[row:row + 1, lanes], (SUBLANES, LANES))
        st_ref[...] = hst0_ref[...]

    for c in _tile_copies(h_hbm, in_buf, in_sem, b, t, slot, tt, True):
        c.wait()
    h = in_buf[slot].reshape(tt, D_MODEL)
    xn = _rms(h, g_ref[...]).astype(BF16)
    sub = lax.broadcasted_iota(jnp.int32, (SUBLANES, LANES), 0)
    u_rnn = jnp.dot(xn, w_in_bf[:, D_POOL:D_POOL + D_RNN], preferred_element_type=F32)
    for g in range(N_SLABS):
        rnn_e[g, CONV_HEAD * SUBLANES:, :] = u_rnn[:, g * LANES:(g + 1) * LANES]
    _fill_heads(rnn_e, rnn_tb, CONV_HEAD, seg, sub)
    xc = []
    for g in range(N_SLABS):
        lanes = slice(g * LANES, (g + 1) * LANES)
        acc = cb_ref[:, lanes]
        for k in range(CONV_WIDTH):
            acc = acc + rnn_e[g, k * SUBLANES:k * SUBLANES + tt, :] * cw_ref[k:k + 1, lanes]
        xc.append(acc)
    xc = jnp.concatenate(xc, axis=-1)

    u_pool = jnp.dot(xn, w_in_bf[:, :D_POOL], preferred_element_type=F32)
    for g in range(N_SLABS):
        pool_e[g, POOL_TAIL * SUBLANES:, :] = u_pool[:, g * LANES:(g + 1) * LANES]
    _fill_heads(pool_e, pool_tb, POOL_TAIL, seg, sub)

    pooled = []
    for g, k in enumerate(POOL_WINDOWS):
        e = pool_e[g]
        w = e
        span = 1
        while span < k:
            w = w[span * SUBLANES:] + w[:-span * SUBLANES]
            span *= 2
        pooled.append(w[-tt:] * (1.0 / k) - e[POOL_TAIL * SUBLANES:])
    a, b_in = _lru_coeffs(xc, gw_ref, rb_ref, ib_ref, lam_ref)
    y_pool = _pool_out(pooled, pw_ref, pb_ref, ps_ref, gng_ref)
    gate = jnp.dot(xn, w_in_bf[:, D_POOL + D_RNN:], preferred_element_type=F32)
    res_pool = jnp.dot(y_pool.astype(BF16), w_out_bf[:D_POOL, :], preferred_element_type=F32)

    hl = ac = None
    for j in range(seg):
        rows = slice(j * SUBLANES, (j + 1) * SUBLANES)
        hl = b_in[rows] if j == 0 else a[rows] * hl + b_in[rows]
        ac = a[rows] if j == 0 else a[rows] * ac
        hl_buf[rows, :] = hl
        ac_buf[rows, :] = ac
    aend_ref[...] = ac
    hend_ref[...] = hl
    c = st_ref[0:1, :]
    for s in range(SUBLANES):
        carry_ref[s:s + 1, :] = c
        c = aend_ref[s:s + 1, :] * c + hend_ref[s:s + 1, :]
    st_ref[0:1, :] = c
    carry = carry_ref[...]
    for j in range(seg):
        rows = slice(j * SUBLANES, (j + 1) * SUBLANES)
        hs_buf[rows, :] = hl_buf[rows, :] + ac_buf[rows, :] * carry

    y_rnn = _rms(hs_buf[...] * _gelu_tanh(gate), gng_ref[:, D_POOL:]).astype(BF16)
    res = h + res_pool + jnp.dot(y_rnn, w_out_bf[D_POOL:, :], preferred_element_type=F32)

    @pl.when(n >= 2)
    def _():
        for c in _tile_copies(out_hbm, out_buf, out_sem, b, t, slot, tt, False):
            c.wait()

    out_buf[slot] = res.reshape(seg, SUBLANES, D_MODEL)
    for c in _tile_copies(out_hbm, out_buf, out_sem, b, t, slot, tt, False):
        c.start()

    @pl.when(n == n_steps - 1)
    def _():
        for c in _tile_copies(out_hbm, out_buf, out_sem, b, t, slot, tt, False):
            c.wait()

        @pl.when(n >= 1)
        def _():
            for c in _tile_copies(out_hbm, out_buf, out_sem, b, t, 1 - slot, tt, False):
                c.wait()


def _const_spec(shape):
    return pl.BlockSpec(shape, lambda *_: (0,) * len(shape), pipeline_mode=pl.Buffered(1))


def _layer_spec(arr, l):
    rest = arr.shape[1:]
    return pl.BlockSpec((None,) + rest, lambda *_: (l,) + (0,) * len(rest), pipeline_mode=pl.Buffered(1))


_MIXER_WEIGHTS = ("mix_g", "w_in", "pool_w", "pool_b", "pool_scale", "conv_w", "conv_b",
                  "gate_w", "gate_r_b", "gate_i_b", "lam", "gn_g", "w_out")
_BF16_WEIGHT_SCRATCH = [pltpu.VMEM((D_MODEL, D_IN_PROJ), BF16), pltpu.VMEM((D_MODEL, D_MODEL), BF16)]


def _prefix_mixer_call(h, params, l):
    _, tt, D = h.shape
    weights = tuple(params[k] for k in _MIXER_WEIGHTS)
    state_shapes = ((POOL_TAIL, D_POOL), (CONV_TAIL, D_RNN), (SUBLANES, D_RNN))
    out, *state = pl.pallas_call(
        functools.partial(_prefix_mixer_kernel, tt),
        grid=(1,),
        in_specs=[_const_spec(h.shape)] + [_layer_spec(a, l) for a in weights],
        out_specs=[pl.BlockSpec(s, lambda i, n=len(s): (0,) * n) for s in (h.shape,) + state_shapes],
        out_shape=[jax.ShapeDtypeStruct(h.shape, F32)] + [jax.ShapeDtypeStruct(s, F32) for s in state_shapes],
        scratch_shapes=[pltpu.VMEM((POOL_TAIL + tt, D_POOL), F32),
                        pltpu.VMEM((CONV_TAIL + tt, D_RNN), F32)] + _BF16_WEIGHT_SCRATCH,
        compiler_params=pltpu.CompilerParams(dimension_semantics=("arbitrary",),
                                             vmem_limit_bytes=VMEM_LIMIT_BYTES),
        name="prefix_mixer",
    )(h, *weights)
    return out, tuple(state)


def _mixer_call(h, state, params, l, *, tt):
    B, T, D = h.shape
    seg = tt // SUBLANES
    assert T % tt == 0 and tt % SUBLANES == 0 and seg >= POOL_TAIL
    weights = tuple(params[k] for k in _MIXER_WEIGHTS)
    hbm = pl.BlockSpec(memory_space=pl.ANY)
    slab = lambda rows: pltpu.VMEM((N_SLABS, rows, LANES), F32)
    small = pltpu.VMEM((SUBLANES, D_RNN), F32)
    tile_buf = pltpu.VMEM((2, seg, SUBLANES, D), F32)
    return pl.pallas_call(
        functools.partial(_mixer_kernel, tt),
        grid=(B, T // tt),
        in_specs=[hbm] + [_const_spec(a.shape) for a in state] + [_layer_spec(a, l) for a in weights],
        out_specs=hbm,
        out_shape=jax.ShapeDtypeStruct((B, T, D), F32),
        scratch_shapes=[tile_buf, tile_buf, pltpu.SemaphoreType.DMA((2,)), pltpu.SemaphoreType.DMA((2,)),
                        slab((POOL_TAIL + seg) * SUBLANES), slab((CONV_HEAD + seg) * SUBLANES),
                        slab(POOL_TAIL * SUBLANES), slab(CONV_HEAD * SUBLANES),
                        pltpu.VMEM((tt, D_RNN), F32), pltpu.VMEM((tt, D_RNN), F32), pltpu.VMEM((tt, D_RNN), F32),
                        small, small, small, small]
        + _BF16_WEIGHT_SCRATCH,
        compiler_params=pltpu.CompilerParams(dimension_semantics=("arbitrary", "arbitrary"),
                                             vmem_limit_bytes=VMEM_LIMIT_BYTES),
        name="mixer",
    )(h, *state, *weights)


def _mlp_kernel(final_norm, h_ref, g_ref, w_up_ref, w_down_ref, fg_ref, out_ref):
    h = h_ref[...]
    xn = _rms(h, g_ref[...]).astype(BF16)
    acc = h
    for c in range(D_FF // FF_CHUNK):
        cols = slice(c * FF_CHUNK, (c + 1) * FF_CHUNK)
        u = jnp.dot(xn, w_up_ref[:, cols].astype(BF16), preferred_element_type=F32)
        u = jnp.maximum(u, 0.0)
        acc = acc + jnp.dot((u * u).astype(BF16), w_down_ref[cols, :].astype(BF16), preferred_element_type=F32)
    if final_norm:
        acc = _rms(acc, fg_ref[...])
    out_ref[...] = acc


def _mlp_call(h, params, l, *, final_norm, tt):
    shape = h.shape
    rows = h.reshape(-1, D_MODEL)
    n = rows.shape[0]
    assert n % tt == 0
    tile = pl.BlockSpec((tt, D_MODEL), lambda i: (i, 0))
    out = pl.pallas_call(
        functools.partial(_mlp_kernel, final_norm),
        grid=(n // tt,),
        in_specs=[tile, _layer_spec(params["mlp_g"], l), _layer_spec(params["w_up"], l),
                  _layer_spec(params["w_down"], l), _const_spec(params["final_g"].shape)],
        out_specs=tile,
        out_shape=jax.ShapeDtypeStruct((n, D_MODEL), F32),
        compiler_params=pltpu.CompilerParams(dimension_semantics=("arbitrary",),
                                             vmem_limit_bytes=VMEM_LIMIT_BYTES),
        name="mlp",
    )(rows, params["mlp_g"], params["w_up"], params["w_down"], params["final_g"])
    return out.reshape(shape)


def _block_diag(w, per_tile):
    L, n, d, _ = w.shape
    w = w.reshape(L, n // per_tile, per_tile, d, d)
    eye = jnp.eye(per_tile, dtype=w.dtype)
    bd = w[:, :, :, :, None, :] * eye[None, None, :, None, :, None]
    return bd.reshape(L, n // per_tile, per_tile * d, per_tile * d)


def kernel(x, meta_tokens, mix_norm_g, w_in, pool_w, pool_b, pool_scale, conv_w, conv_b, gate_r_w, gate_r_b,
           gate_i_w, gate_i_b, lru_lambda, group_norm_g, w_out, mlp_norm_g, w_up, w_down, final_norm_g):
    depth = w_in.shape[0]
    rows = lambda v: v.reshape(depth, 1, -1)
    gate_w = jnp.concatenate([_block_diag(gate_r_w, MXU_DIM // RNN_HEAD_DIM),
                              _block_diag(gate_i_w, MXU_DIM // RNN_HEAD_DIM)], axis=-1)
    params = {
        "mix_g": rows(mix_norm_g), "w_in": w_in,
        "pool_w": _block_diag(pool_w, MXU_DIM // POOL_GROUP_DIM).astype(BF16),
        "pool_b": rows(pool_b), "pool_scale": rows(pool_scale), "conv_w": conv_w, "conv_b": rows(conv_b),
        "gate_w": gate_w.astype(BF16), "gate_r_b": rows(gate_r_b), "gate_i_b": rows(gate_i_b),
        "lam": rows(lru_lambda), "gn_g": rows(group_norm_g), "w_out": w_out, "mlp_g": rows(mlp_norm_g),
        "w_up": w_up, "w_down": w_down, "final_g": final_norm_g.reshape(1, -1),
    }

    hm = meta_tokens.astype(x.dtype)[None]
    states = []
    for l in range(depth):
        hm, st = _prefix_mixer_call(hm, params, l)
        states.append(st)
        if l + 1 < depth:
            hm = _mlp_call(hm, params, l, final_norm=False, tt=N_META)

    h = x
    for l in range(depth):
        h = _mixer_call(h, states[l], params, l, tt=512)
        h = _mlp_call(h, params, l, final_norm=(l + 1 == depth), tt=512)
    return h
```

```python
import functools

import jax
import jax.numpy as jnp
from jax import lax
from jax.experimental import pallas as pl
from jax.experimental.pallas import tpu as pltpu

D_MODEL = 1024
D_POOL = 512
D_RNN = 512
POOL_WINDOWS = (2, 4, 8, 16)
POOL_GROUP_DIM = 128
N_RNN_HEADS = 8
RNN_HEAD_DIM = 64
CONV_WIDTH = 4
LRU_C = 8.0
D_IN_PROJ = D_POOL + 2 * D_RNN
D_FF = 4 * D_MODEL
EPS = 1e-6

SUBLANES = 8
LANES = 128
N_SLABS = D_POOL // LANES
POOL_TAIL = max(POOL_WINDOWS)
CONV_TAIL = SUBLANES
CONV_HEAD = CONV_WIDTH - 1
MXU_DIM = 256
FF_CHUNK = 1024
VMEM_LIMIT_BYTES = 56 * 1024 * 1024

F32 = jnp.float32
BF16 = jnp.bfloat16


def _rms(x, g):
    return x * lax.rsqrt(jnp.mean(x * x, axis=-1, keepdims=True) + EPS) * g


def _sigmoid(x):
    return 0.5 * jnp.tanh(0.5 * x) + 0.5


def _gelu_tanh(x):
    return 0.5 * x * (1.0 + jnp.tanh(0.7978845608028654 * (x + 0.044715 * (x * x * x))))


def _in_proj(h, g_ref, w_in_bf):
    xn = _rms(h, g_ref[...])
    return jnp.dot(xn.astype(BF16), w_in_bf[...], preferred_element_type=F32)


def _pool_out(pooled, pw_ref, pb_ref, ps_ref, gng_ref):
    mapped = []
    for p in range(D_POOL // MXU_DIM):
        pg = jnp.concatenate(pooled[2 * p:2 * p + 2], axis=-1).astype(BF16)
        mapped.append(jnp.dot(pg, pw_ref[p], preferred_element_type=F32))
    y_pool = (jnp.concatenate(mapped, axis=-1) + pb_ref[...]) * ps_ref[...]
    return _rms(y_pool, gng_ref[:, :D_POOL])


def _lru_coeffs(xc, gw_ref, rb_ref, ib_ref, lam_ref):
    xcb = xc.astype(BF16)
    g0 = jnp.dot(xcb[:, :MXU_DIM], gw_ref[0], preferred_element_type=F32)
    g1 = jnp.dot(xcb[:, MXU_DIM:], gw_ref[1], preferred_element_type=F32)
    r = _sigmoid(jnp.concatenate([g0[:, :MXU_DIM], g1[:, :MXU_DIM]], axis=-1) + rb_ref[...])
    i = _sigmoid(jnp.concatenate([g0[:, MXU_DIM:], g1[:, MXU_DIM:]], axis=-1) + ib_ref[...])
    lam = lam_ref[...]
    softplus_neg_lam = jnp.maximum(-lam, 0.0) + jnp.log1p(jnp.exp(-jnp.abs(lam)))
    log_a = (-LRU_C * softplus_neg_lam) * r
    a = jnp.exp(log_a)
    b = jnp.sqrt(jnp.tanh(-log_a) * (1.0 + a * a)) * (i * xc)
    return a, b


def _out_proj(h, y_pool, h_lru, gate, gng_ref, w_out_bf):
    y_rnn = _rms(h_lru * _gelu_tanh(gate), gng_ref[:, D_POOL:])
    y = jnp.concatenate([y_pool, y_rnn], axis=-1).astype(BF16)
    return h + jnp.dot(y, w_out_bf[...], preferred_element_type=F32)


def _prefix_mixer_kernel(tt,
                         h_ref, g_ref, w_in_ref, pw_ref, pb_ref, ps_ref, cw_ref, cb_ref, gw_ref, rb_ref, ib_ref,
                         lam_ref, gng_ref, w_out_ref,
                         out_ref, ptail_ref, ctail_ref, hst_ref,
                         pool_buf, rnn_buf, w_in_bf, w_out_bf):
    w_in_bf[...] = w_in_ref[...].astype(BF16)
    w_out_bf[...] = w_out_ref[...].astype(BF16)
    pool_buf[0:POOL_TAIL, :] = jnp.zeros((POOL_TAIL, D_POOL), F32)
    rnn_buf[0:CONV_TAIL, :] = jnp.zeros((CONV_TAIL, D_RNN), F32)

    h = h_ref[0]
    proj = _in_proj(h, g_ref, w_in_bf)
    pool_buf[POOL_TAIL:POOL_TAIL + tt, :] = proj[:, :D_POOL]
    rnn_buf[CONV_TAIL:CONV_TAIL + tt, :] = proj[:, D_POOL:D_POOL + D_RNN]
    gate = proj[:, D_POOL + D_RNN:]

    frame = lax.broadcasted_iota(jnp.int32, (tt, 1), 0)
    pooled = []
    for g, k in enumerate(POOL_WINDOWS):
        sl = slice(g * POOL_GROUP_DIM, (g + 1) * POOL_GROUP_DIM)
        u = pool_buf[POOL_TAIL:POOL_TAIL + tt, sl]
        s = u
        for j in range(1, k):
            s = s + pool_buf[POOL_TAIL - j:POOL_TAIL - j + tt, sl]
        pooled.append(s * (1.0 / jnp.minimum(frame + 1, k).astype(F32)) - u)
    y_pool = _pool_out(pooled, pw_ref, pb_ref, ps_ref, gng_ref)

    xc = cb_ref[...]
    for k in range(CONV_WIDTH):
        off = CONV_TAIL - CONV_HEAD + k
        xc = xc + rnn_buf[off:off + tt, :] * cw_ref[k:k + 1, :]
    a, b = _lru_coeffs(xc, gw_ref, rb_ref, ib_ref, lam_ref)
    state = jnp.zeros((1, D_RNN), F32)
    h_rows = []
    for j in range(tt):
        state = a[j:j + 1, :] * state + b[j:j + 1, :]
        h_rows.append(state)
    h_lru = jnp.concatenate(h_rows, axis=0)

    out_ref[0] = _out_proj(h, y_pool, h_lru, gate, gng_ref, w_out_bf)
    ptail_ref[...] = pool_buf[tt:tt + POOL_TAIL, :]
    ctail_ref[...] = rnn_buf[tt:tt + CONV_TAIL, :]
    hst_ref[...] = jnp.broadcast_to(state, (SUBLANES, D_RNN))


def _fill_heads(e_ref, tb_ref, n_head, seg, sub):
    for g in range(N_SLABS):
        for v in range(n_head):
            cur = e_ref[g, (seg + v) * SUBLANES:(seg + v + 1) * SUBLANES, :]
            prev = tb_ref[g, v * SUBLANES:(v + 1) * SUBLANES, :]
            e_ref[g, v * SUBLANES:(v + 1) * SUBLANES, :] = pltpu.roll(
                jnp.where(sub == SUBLANES - 1, prev, cur), 1, 0)
        tb_ref[g] = e_ref[g, seg * SUBLANES:(seg + n_head) * SUBLANES, :]


def _tile_copies(hbm, buf, sem, b, t, slot, tt, to_vmem):
    seg = tt // SUBLANES
    copies = []
    for s in range(SUBLANES):
        rows = hbm.at[b, pl.ds(t * tt + s * seg, seg), :]
        regs = buf.at[slot, :, s, :]
        copies.append(pltpu.make_async_copy(rows, regs, sem.at[slot]) if to_vmem
                      else pltpu.make_async_copy(regs, rows, sem.at[slot]))
    return copies


def _mixer_kernel(tt,
                  h_hbm, ptail0_ref, ctail0_ref, hst0_ref,
                  g_ref, w_in_ref, pw_ref, pb_ref, ps_ref, cw_ref, cb_ref, gw_ref, rb_ref, ib_ref,
                  lam_ref, gng_ref, w_out_ref,
                  out_hbm,
                  in_buf, out_buf, in_sem, out_sem,
                  pool_e, rnn_e, pool_tb, rnn_tb, hl_buf, ac_buf, hs_buf,
                  aend_ref, hend_ref, carry_ref, st_ref, w_in_bf, w_out_bf):
    seg = tt // SUBLANES
    b = pl.program_id(0)
    t = pl.program_id(1)
    n_t = pl.num_programs(1)
    n = b * n_t + t
    n_steps = pl.num_programs(0) * n_t
    slot = n % 2

    @pl.when(n == 0)
    def _():
        for c in _tile_copies(h_hbm, in_buf, in_sem, b, t, slot, tt, True):
            c.start()

    @pl.when(n + 1 < n_steps)
    def _():
        wrap = t + 1 == n_t
        for c in _tile_copies(h_hbm, in_buf, in_sem, jnp.where(wrap, b + 1, b), jnp.where(wrap, 0, t + 1),
                              1 - slot, tt, True):
            c.start()

    @pl.when(t == 0)
    def _():
        w_in_bf[...] = w_in_ref[...].astype(BF16)
        w_out_bf[...] = w_out_ref[...].astype(BF16)
        for g in range(N_SLABS):
            lanes = slice(g * LANES, (g + 1) * LANES)
            for v in range(POOL_TAIL):
                pool_tb[g, v * SUBLANES:(v + 1) * SUBLANES, :] = jnp.broadcast_to(
                    ptail0_ref[v:v + 1, lanes], (SUBLANES, LANES))
            for v in range(CONV_HEAD):
                row = CONV_TAIL - CONV_HEAD + v
                rnn_tb[g, v * SUBLANES:(v + 1) * SUBLANES, :] = jnp.broadcast_to(
                    ctail0_ref[row:row + 1, lanes], (SUBLANES, LANES))
        st_ref[...] = hst0_ref[...]

    for c in _tile_copies(h_hbm, in_buf, in_sem, b, t, slot, tt, True):
        c.wait()
    h = in_buf[slot].reshape(tt, D_MODEL)
    xn = _rms(h, g_ref[...]).astype(BF16)
    sub = lax.broadcasted_iota(jnp.int32, (SUBLANES, LANES), 0)
    u_rnn = jnp.dot(xn, w_in_bf[:, D_POOL:D_POOL + D_RNN], preferred_element_type=F32)
    for g in range(N_SLABS):
        rnn_e[g, CONV_HEAD * SUBLANES:, :] = u_rnn[:, g * LANES:(g + 1) * LANES]
    _fill_heads(rnn_e, rnn_tb, CONV_HEAD, seg, sub)
    xc = []
    for g in range(N_SLABS):
        lanes = slice(g * LANES, (g + 1) * LANES)
        acc = cb_ref[:, lanes]
        for k in range(CONV_WIDTH):
            acc = acc + rnn_e[g, k * SUBLANES:k * SUBLANES + tt, :] * cw_ref[k:k + 1, lanes]
        xc.append(acc)
    xc = jnp.concatenate(xc, axis=-1)

    u_pool = jnp.dot(xn, w_in_bf[:, :D_POOL], preferred_element_type=F32)
    for g in range(N_SLABS):
        pool_e[g, POOL_TAIL * SUBLANES:, :] = u_pool[:, g * LANES:(g + 1) * LANES]
    _fill_heads(pool_e, pool_tb, POOL_TAIL, seg, sub)

    pooled = []
    for g, k in enumerate(POOL_WINDOWS):
        e = pool_e[g]
        w = e
        span = 1
        while span < k:
            w = w[span * SUBLANES:] + w[:-span * SUBLANES]
            span *= 2
        pooled.append(w[-tt:] * (1.0 / k) - e[POOL_TAIL * SUBLANES:])
    a, b_in = _lru_coeffs(xc, gw_ref, rb_ref, ib_ref, lam_ref)
    y_pool = _pool_out(pooled, pw_ref, pb_ref, ps_ref, gng_ref)
    gate = jnp.dot(xn, w_in_bf[:, D_POOL + D_RNN:], preferred_element_type=F32)
    res_pool = jnp.dot(y_pool.astype(BF16), w_out_bf[:D_POOL, :], preferred_element_type=F32)

    hl = ac = None
    for j in range(seg):
        rows = slice(j * SUBLANES, (j + 1) * SUBLANES)
        hl = b_in[rows] if j == 0 else a[rows] * hl + b_in[rows]
        ac = a[rows] if j == 0 else a[rows] * ac
        hl_buf[rows, :] = hl
        ac_buf[rows, :] = ac
    aend_ref[...] = ac
    hend_ref[...] = hl
    c = st_ref[0:1, :]
    for s in range(SUBLANES):
        carry_ref[s:s + 1, :] = c
        c = aend_ref[s:s + 1, :] * c + hend_ref[s:s + 1, :]
    st_ref[0:1, :] = c
    carry = carry_ref[...]
    for j in range(seg):
        rows = slice(j * SUBLANES, (j + 1) * SUBLANES)
        hs_buf[rows, :] = hl_buf[rows, :] + ac_buf[rows, :] * carry

    y_rnn = _rms(hs_buf[...] * _gelu_tanh(gate), gng_ref[:, D_POOL:]).astype(BF16)
    res = h + res_pool + jnp.dot(y_rnn, w_out_bf[D_POOL:, :], preferred_element_type=F32)

    @pl.when(n >= 2)
    def _():
        for c in _tile_copies(out_hbm, out_buf, out_sem, b, t, slot, tt, False):
            c.wait()

    out_buf[slot] = res.reshape(seg, SUBLANES, D_MODEL)
    for c in _tile_copies(out_hbm, out_buf, out_sem, b, t, slot, tt, False):
        c.start()

    @pl.when(n == n_steps - 1)
    def _():
        for c in _tile_copies(out_hbm, out_buf, out_sem, b, t, slot, tt, False):
            c.wait()

        @pl.when(n >= 1)
        def _():
            for c in _tile_copies(out_hbm, out_buf, out_sem, b, t, 1 - slot, tt, False):
                c.wait()


def _const_spec(shape):
    return pl.BlockSpec(shape, lambda *_: (0,) * len(shape), pipeline_mode=pl.Buffered(1))


def _layer_spec(arr, l):
    rest = arr.shape[1:]
    return pl.BlockSpec((None,) + rest, lambda *_: (l,) + (0,) * len(rest), pipeline_mode=pl.Buffered(1))


_MIXER_WEIGHTS = ("mix_g", "w_in", "pool_w", "pool_b", "pool_scale", "conv_w", "conv_b",
                  "gate_w", "gate_r_b", "gate_i_b", "lam", "gn_g", "w_out")
_BF16_WEIGHT_SCRATCH = [pltpu.VMEM((D_MODEL, D_IN_PROJ), BF16), pltpu.VMEM((D_MODEL, D_MODEL), BF16)]


def _prefix_mixer_call(h, params, l):
    _, tt, D = h.shape
    weights = tuple(params[k] for k in _MIXER_WEIGHTS)
    state_shapes = ((POOL_TAIL, D_POOL), (CONV_TAIL, D_RNN), (SUBLANES, D_RNN))
    out, *state = pl.pallas_call(
        functools.partial(_prefix_mixer_kernel, tt),
        grid=(1,),
        in_specs=[_const_spec(h.shape)] + [_layer_spec(a, l) for a in weights],
        out_specs=[pl.BlockSpec(s, lambda i, n=len(s): (0,) * n) for s in (h.shape,) + state_shapes],
        out_shape=[jax.ShapeDtypeStruct(h.shape, F32)] + [jax.ShapeDtypeStruct(s, F32) for s in state_shapes],
        scratch_shapes=[pltpu.VMEM((POOL_TAIL + tt, D_POOL), F32),
                        pltpu.VMEM((CONV_TAIL + tt, D_RNN), F32)] + _BF16_WEIGHT_SCRATCH,
        compiler_params=pltpu.CompilerParams(dimension_semantics=("arbitrary",),
                                             vmem_limit_bytes=VMEM_LIMIT_BYTES),
        name="prefix_mixer",
    )(h, *weights)
    return out, tuple(state)


def _mixer_call(h, state, params, l, *, tt):
    B, T, D = h.shape
    seg = tt // SUBLANES
    assert T % tt == 0 and tt % SUBLANES == 0 and seg >= POOL_TAIL
    weights = tuple(params[k] for k in _MIXER_WEIGHTS)
    hbm = pl.BlockSpec(memory_space=pl.ANY)
    slab = lambda rows: pltpu.VMEM((N_SLABS, rows, LANES), F32)
    small = pltpu.VMEM((SUBLANES, D_RNN), F32)
    tile_buf = pltpu.VMEM((2, seg, SUBLANES, D), F32)
    return pl.pallas_call(
        functools.partial(_mixer_kernel, tt),
        grid=(B, T // tt),
        in_specs=[hbm] + [_const_spec(a.shape) for a in state] + [_layer_spec(a, l) for a in weights],
        out_specs=hbm,
        out_shape=jax.ShapeDtypeStruct((B, T, D), F32),
        scratch_shapes=[tile_buf, tile_buf, pltpu.SemaphoreType.DMA((2,)), pltpu.SemaphoreType.DMA((2,)),
                        slab((POOL_TAIL + seg) * SUBLANES), slab((CONV_HEAD + seg) * SUBLANES),
                        slab(POOL_TAIL * SUBLANES), slab(CONV_HEAD * SUBLANES),
                        pltpu.VMEM((tt, D_RNN), F32), pltpu.VMEM((tt, D_RNN), F32), pltpu.VMEM((tt, D_RNN), F32),
                        small, small, small, small]
        + _BF16_WEIGHT_SCRATCH,
        compiler_params=pltpu.CompilerParams(dimension_semantics=("arbitrary", "arbitrary"),
                                             vmem_limit_bytes=VMEM_LIMIT_BYTES),
        name="mixer",
    )(h, *state, *weights)


def _mlp_rows(h, g_ref, w_up_ref, w_down_ref):
    xn = _rms(h, g_ref[...]).astype(BF16)
    acc = h
    for c in range(D_FF // FF_CHUNK):
        cols = slice(c * FF_CHUNK, (c + 1) * FF_CHUNK)
        u = jnp.dot(xn, w_up_ref[:, cols].astype(BF16), preferred_element_type=F32)
        u = jnp.maximum(u, 0.0)
        acc = acc + jnp.dot((u * u).astype(BF16), w_down_ref[cols, :].astype(BF16), preferred_element_type=F32)
    return acc


def _mlp_final_kernel(h_ref, g_ref, w_up_ref, w_down_ref, fg_ref, out_ref):
    out_ref[...] = _rms(_mlp_rows(h_ref[...], g_ref, w_up_ref, w_down_ref), fg_ref[...])


def _mlp_with_prefix_kernel(h_ref, hp_ref, g_ref, w_up_ref, w_down_ref, out_ref, outp_ref):
    i = pl.program_id(0)

    @pl.when(i == 0)
    def _():
        n_p = hp_ref.shape[0]
        acc = _mlp_rows(jnp.concatenate([hp_ref[...], h_ref[...]], axis=0), g_ref, w_up_ref, w_down_ref)
        outp_ref[...] = acc[:n_p]
        out_ref[...] = acc[n_p:]

    @pl.when(i > 0)
    def _():
        out_ref[...] = _mlp_rows(h_ref[...], g_ref, w_up_ref, w_down_ref)


def _mlp_call(h, params, l, *, tt, prefix=None):
    shape = h.shape
    rows = h.reshape(-1, D_MODEL)
    n = rows.shape[0]
    assert n % tt == 0
    tile = pl.BlockSpec((tt, D_MODEL), lambda i: (i, 0))
    weights = [params["mlp_g"], params["w_up"], params["w_down"]]
    weight_specs = [_layer_spec(a, l) for a in weights]
    compiler_params = pltpu.CompilerParams(dimension_semantics=("arbitrary",), vmem_limit_bytes=VMEM_LIMIT_BYTES)
    if prefix is None:
        out = pl.pallas_call(
            _mlp_final_kernel,
            grid=(n // tt,),
            in_specs=[tile] + weight_specs + [_const_spec(params["final_g"].shape)],
            out_specs=tile,
            out_shape=jax.ShapeDtypeStruct((n, D_MODEL), F32),
            compiler_params=compiler_params,
            name="mlp_final",
        )(rows, *weights, params["final_g"])
        return out.reshape(shape)
    assert prefix.shape[0] % SUBLANES == 0
    out, prefix_out = pl.pallas_call(
        _mlp_with_prefix_kernel,
        grid=(n // tt,),
        in_specs=[tile, _const_spec(prefix.shape)] + weight_specs,
        out_specs=[tile, pl.BlockSpec(prefix.shape, lambda i: (0, 0))],
        out_shape=[jax.ShapeDtypeStruct((n, D_MODEL), F32), jax.ShapeDtypeStruct(prefix.shape, F32)],
        compiler_params=compiler_params,
        name="mlp",
    )(rows, prefix, *weights)
    return out.reshape(shape), prefix_out


def _block_diag(w, per_tile):
    L, n, d, _ = w.shape
    w = w.reshape(L, n // per_tile, per_tile, d, d)
    eye = jnp.eye(per_tile, dtype=w.dtype)
    bd = w[:, :, :, :, None, :] * eye[None, None, :, None, :, None]
    return bd.reshape(L, n // per_tile, per_tile * d, per_tile * d)


def kernel(x, meta_tokens, mix_norm_g, w_in, pool_w, pool_b, pool_scale, conv_w, conv_b, gate_r_w, gate_r_b,
           gate_i_w, gate_i_b, lru_lambda, group_norm_g, w_out, mlp_norm_g, w_up, w_down, final_norm_g):
    depth = w_in.shape[0]
    rows = lambda v: v.reshape(depth, 1, -1)
    gate_w = jnp.concatenate([_block_diag(gate_r_w, MXU_DIM // RNN_HEAD_DIM),
                              _block_diag(gate_i_w, MXU_DIM // RNN_HEAD_DIM)], axis=-1)
    params = {
        "mix_g": rows(mix_norm_g), "w_in": w_in,
        "pool_w": _block_diag(pool_w, MXU_DIM // POOL_GROUP_DIM).astype(BF16),
        "pool_b": rows(pool_b), "pool_scale": rows(pool_scale), "conv_w": conv_w, "conv_b": rows(conv_b),
        "gate_w": gate_w.astype(BF16), "gate_r_b": rows(gate_r_b), "gate_i_b": rows(gate_i_b),
        "lam": rows(lru_lambda), "gn_g": rows(group_norm_g), "w_out": w_out, "mlp_g": rows(mlp_norm_g),
        "w_up": w_up, "w_down": w_down, "final_g": final_norm_g.reshape(1, -1),
    }

    hm = meta_tokens.astype(x.dtype)
    h = x
    for l in range(depth):
        hm, state = _prefix_mixer_call(hm[None], params, l)
        h = _mixer_call(h, state, params, l, tt=512)
        if l + 1 < depth:
            h, hm = _mlp_call(h, params, l, tt=512, prefix=hm[0])
        else:
            h = _mlp_call(h, params, l, tt=512)
    return h
```

```python
import functools

import jax
import jax.numpy as jnp
from jax import lax
from jax.experimental import pallas as pl
from jax.experimental.pallas import tpu as pltpu

D_MODEL = 1024
D_POOL = 512
D_RNN = 512
POOL_WINDOWS = (2, 4, 8, 16)
POOL_GROUP_DIM = 128
N_RNN_HEADS = 8
RNN_HEAD_DIM = 64
CONV_WIDTH = 4
LRU_C = 8.0
D_IN_PROJ = D_POOL + 2 * D_RNN
D_FF = 4 * D_MODEL
EPS = 1e-6

SUBLANES = 8
LANES = 128
N_SLABS = D_POOL // LANES
POOL_TAIL = max(POOL_WINDOWS)
CONV_TAIL = SUBLANES
CONV_HEAD = CONV_WIDTH - 1
MXU_DIM = 256
FF_CHUNK = 1024
VMEM_LIMIT_BYTES = 56 * 1024 * 1024

F32 = jnp.float32
BF16 = jnp.bfloat16


def _rms(x, g):
    return x * lax.rsqrt(jnp.mean(x * x, axis=-1, keepdims=True) + EPS) * g


def _sigmoid(x):
    return 0.5 * jnp.tanh(0.5 * x) + 0.5


def _gelu_tanh(x):
    return 0.5 * x * (1.0 + jnp.tanh(0.7978845608028654 * (x + 0.044715 * (x * x * x))))


def _in_proj(h, g_ref, w_in_bf):
    xn = _rms(h, g_ref[...])
    return jnp.dot(xn.astype(BF16), w_in_bf[...], preferred_element_type=F32)


def _pool_out(pooled, pw_ref, pb_ref, ps_ref, gng_ref):
    mapped = []
    for p in range(D_POOL // MXU_DIM):
        pg = jnp.concatenate(pooled[2 * p:2 * p + 2], axis=-1).astype(BF16)
        mapped.append(jnp.dot(pg, pw_ref[p], preferred_element_type=F32))
    y_pool = (jnp.concatenate(mapped, axis=-1) + pb_ref[...]) * ps_ref[...]
    return _rms(y_pool, gng_ref[:, :D_POOL])


def _lru_coeffs(xc, gw_ref, rb_ref, ib_ref, lam_ref):
    xcb = xc.astype(BF16)
    g0 = jnp.dot(xcb[:, :MXU_DIM], gw_ref[0], preferred_element_type=F32)
    g1 = jnp.dot(xcb[:, MXU_DIM:], gw_ref[1], preferred_element_type=F32)
    r = _sigmoid(jnp.concatenate([g0[:, :MXU_DIM], g1[:, :MXU_DIM]], axis=-1) + rb_ref[...])
    i = _sigmoid(jnp.concatenate([g0[:, MXU_DIM:], g1[:, MXU_DIM:]], axis=-1) + ib_ref[...])
    lam = lam_ref[...]
    softplus_neg_lam = jnp.maximum(-lam, 0.0) + jnp.log1p(jnp.exp(-jnp.abs(lam)))
    log_a = (-LRU_C * softplus_neg_lam) * r
    a = jnp.exp(log_a)
    b = jnp.sqrt(jnp.tanh(-log_a) * (1.0 + a * a)) * (i * xc)
    return a, b


def _out_proj(h, y_pool, h_lru, gate, gng_ref, w_out_bf):
    y_rnn = _rms(h_lru * _gelu_tanh(gate), gng_ref[:, D_POOL:])
    y = jnp.concatenate([y_pool, y_rnn], axis=-1).astype(BF16)
    return h + jnp.dot(y, w_out_bf[...], preferred_element_type=F32)


def _prefix_mixer_kernel(tt,
                         h_ref, g_ref, w_in_ref, pw_ref, pb_ref, ps_ref, cw_ref, cb_ref, gw_ref, rb_ref, ib_ref,
                         lam_ref, gng_ref, w_out_ref,
                         out_ref, ptail_ref, ctail_ref, hst_ref,
                         pool_buf, rnn_buf, w_in_bf, w_out_bf):
    w_in_bf[...] = w_in_ref[...].astype(BF16)
    w_out_bf[...] = w_out_ref[...].astype(BF16)
    pool_buf[0:POOL_TAIL, :] = jnp.zeros((POOL_TAIL, D_POOL), F32)
    rnn_buf[0:CONV_TAIL, :] = jnp.zeros((CONV_TAIL, D_RNN), F32)

    h = h_ref[0]
    proj = _in_proj(h, g_ref, w_in_bf)
    pool_buf[POOL_TAIL:POOL_TAIL + tt, :] = proj[:, :D_POOL]
    rnn_buf[CONV_TAIL:CONV_TAIL + tt, :] = proj[:, D_POOL:D_POOL + D_RNN]
    gate = proj[:, D_POOL + D_RNN:]

    frame = lax.broadcasted_iota(jnp.int32, (tt, 1), 0)
    pooled = []
    for g, k in enumerate(POOL_WINDOWS):
        sl = slice(g * POOL_GROUP_DIM, (g + 1) * POOL_GROUP_DIM)
        u = pool_buf[POOL_TAIL:POOL_TAIL + tt, sl]
        s = u
        for j in range(1, k):
            s = s + pool_buf[POOL_TAIL - j:POOL_TAIL - j + tt, sl]
        pooled.append(s * (1.0 / jnp.minimum(frame + 1, k).astype(F32)) - u)
    y_pool = _pool_out(pooled, pw_ref, pb_ref, ps_ref, gng_ref)

    xc = cb_ref[...]
    for k in range(CONV_WIDTH):
        off = CONV_TAIL - CONV_HEAD + k
        xc = xc + rnn_buf[off:off + tt, :] * cw_ref[k:k + 1, :]
    a, b = _lru_coeffs(xc, gw_ref, rb_ref, ib_ref, lam_ref)
    state = jnp.zeros((1, D_RNN), F32)
    h_rows = []
    for j in range(tt):
        state = a[j:j + 1, :] * state + b[j:j + 1, :]
        h_rows.append(state)
    h_lru = jnp.concatenate(h_rows, axis=0)

    out_ref[0] = _out_proj(h, y_pool, h_lru, gate, gng_ref, w_out_bf)
    ptail_ref[...] = pool_buf[tt:tt + POOL_TAIL, :]
    ctail_ref[...] = rnn_buf[tt:tt + CONV_TAIL, :]
    hst_ref[...] = jnp.broadcast_to(state, (SUBLANES, D_RNN))


def _fill_heads(e_ref, tb_ref, n_head, seg, sub):
    for g in range(N_SLABS):
        for v in range(n_head):
            cur = e_ref[g, (seg + v) * SUBLANES:(seg + v + 1) * SUBLANES, :]
            prev = tb_ref[g, v * SUBLANES:(v + 1) * SUBLANES, :]
            e_ref[g, v * SUBLANES:(v + 1) * SUBLANES, :] = pltpu.roll(
                jnp.where(sub == SUBLANES - 1, prev, cur), 1, 0)
        tb_ref[g] = e_ref[g, seg * SUBLANES:(seg + n_head) * SUBLANES, :]


def _tile_copies(hbm, buf, sem, b, t, slot, tt, to_vmem):
    seg = tt // SUBLANES
    copies = []
    for s in range(SUBLANES):
        rows = hbm.at[b, pl.ds(t * tt + s * seg, seg), :]
        regs = buf.at[slot, :, s, :]
        copies.append(pltpu.make_async_copy(rows, regs, sem.at[slot]) if to_vmem
                      else pltpu.make_async_copy(regs, rows, sem.at[slot]))
    return copies


def _mixer_kernel(tt,
                  h_hbm, ptail0_ref, ctail0_ref, hst0_ref,
                  g_ref, w_in_ref, pw_ref, pb_ref, ps_ref, cw_ref, cb_ref, gw_ref, rb_ref, ib_ref,
                  lam_ref, gng_ref, w_out_ref,
                  out_hbm,
                  in_buf, out_buf, in_sem, out_sem,
                  pool_e, rnn_e, pool_tb, rnn_tb, hl_buf, ac_buf, hs_buf,
                  aend_ref, hend_ref, carry_ref, st_ref, w_in_bf, w_out_bf):
    seg = tt // SUBLANES
    b = pl.program_id(0)
    t = pl.program_id(1)
    n_t = pl.num_programs(1)
    n = b * n_t + t
    n_steps = pl.num_programs(0) * n_t
    slot = n % 2

    @pl.when(n == 0)
    def _():
        for c in _tile_copies(h_hbm, in_buf, in_sem, b, t, slot, tt, True):
            c.start()

    @pl.when(n + 1 < n_steps)
    def _():
        wrap = t + 1 == n_t
        for c in _tile_copies(h_hbm, in_buf, in_sem, jnp.where(wrap, b + 1, b), jnp.where(wrap, 0, t + 1),
                              1 - slot, tt, True):
            c.start()

    @pl.when(t == 0)
    def _():
        w_in_bf[...] = w_in_ref[...].astype(BF16)
        w_out_bf[...] = w_out_ref[...].astype(BF16)
        for g in range(N_SLABS):
            lanes = slice(g * LANES, (g + 1) * LANES)
            for v in range(POOL_TAIL):
                pool_tb[g, v * SUBLANES:(v + 1) * SUBLANES, :] = jnp.broadcast_to(
                    ptail0_ref[v:v + 1, lanes], (SUBLANES, LANES))
            for v in range(CONV_HEAD):
                row = CONV_TAIL - CONV_HEAD + v
                rnn_tb[g, v * SUBLANES:(v + 1) * SUBLANES, :] = jnp.broadcast_to(
                    ctail0_ref[row:row + 1, lanes], (SUBLANES, LANES))
        st_ref[...] = hst0_ref[...]

    for c in _tile_copies(h_hbm, in_buf, in_sem, b, t, slot, tt, True):
        c.wait()
    h = in_buf[slot].reshape(tt, D_MODEL)
    xn = _rms(h, g_ref[...]).astype(BF16)
    sub = lax.broadcasted_iota(jnp.int32, (SUBLANES, LANES), 0)
    u_rnn = jnp.dot(xn, w_in_bf[:, D_POOL:D_POOL + D_RNN], preferred_element_type=F32)
    for g in range(N_SLABS):
        rnn_e[g, CONV_HEAD * SUBLANES:, :] = u_rnn[:, g * LANES:(g + 1) * LANES]
    _fill_heads(rnn_e, rnn_tb, CONV_HEAD, seg, sub)
    xc = []
    for g in range(N_SLABS):
        lanes = slice(g * LANES, (g + 1) * LANES)
        acc = cb_ref[:, lanes]
        for k in range(CONV_WIDTH):
            acc = acc + rnn_e[g, k * SUBLANES:k * SUBLANES + tt, :] * cw_ref[k:k + 1, lanes]
        xc.append(acc)
    xc = jnp.concatenate(xc, axis=-1)

    u_pool = jnp.dot(xn, w_in_bf[:, :D_POOL], preferred_element_type=F32)
    for g in range(N_SLABS):
        pool_e[g, POOL_TAIL * SUBLANES:, :] = u_pool[:, g * LANES:(g + 1) * LANES]
    _fill_heads(pool_e, pool_tb, POOL_TAIL, seg, sub)

    pooled = []
    for g, k in enumerate(POOL_WINDOWS):
        e = pool_e[g]
        w = e
        span = 1
        while span < k:
            w = w[span * SUBLANES:] + w[:-span * SUBLANES]
            span *= 2
        pooled.append(w[-tt:] * (1.0 / k) - e[POOL_TAIL * SUBLANES:])
    a, b_in = _lru_coeffs(xc, gw_ref, rb_ref, ib_ref, lam_ref)
    y_pool = _pool_out(pooled, pw_ref, pb_ref, ps_ref, gng_ref)
    gate = jnp.dot(xn, w_in_bf[:, D_POOL + D_RNN:], preferred_element_type=F32)
    res_pool = jnp.dot(y_pool.astype(BF16), w_out_bf[:D_POOL, :], preferred_element_type=F32)

    hl = ac = None
    for j in range(seg):
        rows = slice(j * SUBLANES, (j + 1) * SUBLANES)
        hl = b_in[rows] if j == 0 else a[rows] * hl + b_in[rows]
        ac = a[rows] if j == 0 else a[rows] * ac
        hl_buf[rows, :] = hl
        ac_buf[rows, :] = ac
    aend_ref[...] = ac
    hend_ref[...] = hl
    c = st_ref[0:1, :]
    for s in range(SUBLANES):
        carry_ref[s:s + 1, :] = c
        c = aend_ref[s:s + 1, :] * c + hend_ref[s:s + 1, :]
    st_ref[0:1, :] = c
    carry = carry_ref[...]
    for j in range(seg):
        rows = slice(j * SUBLANES, (j + 1) * SUBLANES)
        hs_buf[rows, :] = hl_buf[rows, :] + ac_buf[rows, :] * carry

    y_rnn = _rms(hs_buf[...] * _gelu_tanh(gate), gng_ref[:, D_POOL:]).astype(BF16)
    res = h + res_pool + jnp.dot(y_rnn, w_out_bf[D_POOL:, :], preferred_element_type=F32)

    @pl.when(n >= 2)
    def _():
        for c in _tile_copies(out_hbm, out_buf, out_sem, b, t, slot, tt, False):
            c.wait()

    out_buf[slot] = res.reshape(seg, SUBLANES, D_MODEL)
    for c in _tile_copies(out_hbm, out_buf, out_sem, b, t, slot, tt, False):
        c.start()

    @pl.when(n == n_steps - 1)
    def _():
        for c in _tile_copies(out_hbm, out_buf, out_sem, b, t, slot, tt, False):
            c.wait()

        @pl.when(n >= 1)
        def _():
            for c in _tile_copies(out_hbm, out_buf, out_sem, b, t, 1 - slot, tt, False):
                c.wait()


def _const_spec(shape):
    return pl.BlockSpec(shape, lambda *_: (0,) * len(shape), pipeline_mode=pl.Buffered(1))


def _layer_spec(arr, l):
    rest = arr.shape[1:]
    return pl.BlockSpec((None,) + rest, lambda *_: (l,) + (0,) * len(rest), pipeline_mode=pl.Buffered(1))


_MIXER_WEIGHTS = ("mix_g", "w_in", "pool_w", "pool_b", "pool_scale", "conv_w", "conv_b",
                  "gate_w", "gate_r_b", "gate_i_b", "lam", "gn_g", "w_out")
_BF16_WEIGHT_SCRATCH = [pltpu.VMEM((D_MODEL, D_IN_PROJ), BF16), pltpu.VMEM((D_MODEL, D_MODEL), BF16)]


def _prefix_mixer_call(h, params, l):
    _, tt, D = h.shape
    weights = tuple(params[k] for k in _MIXER_WEIGHTS)
    state_shapes = ((POOL_TAIL, D_POOL), (CONV_TAIL, D_RNN), (SUBLANES, D_RNN))
    out, *state = pl.pallas_call(
        functools.partial(_prefix_mixer_kernel, tt),
        grid=(1,),
        in_specs=[_const_spec(h.shape)] + [_layer_spec(a, l) for a in weights],
        out_specs=[pl.BlockSpec(s, lambda i, n=len(s): (0,) * n) for s in (h.shape,) + state_shapes],
        out_shape=[jax.ShapeDtypeStruct(h.shape, F32)] + [jax.ShapeDtypeStruct(s, F32) for s in state_shapes],
        scratch_shapes=[pltpu.VMEM((POOL_TAIL + tt, D_POOL), F32),
                        pltpu.VMEM((CONV_TAIL + tt, D_RNN), F32)] + _BF16_WEIGHT_SCRATCH,
        compiler_params=pltpu.CompilerParams(dimension_semantics=("arbitrary",),
                                             vmem_limit_bytes=VMEM_LIMIT_BYTES),
        name="prefix_mixer",
    )(h, *weights)
    return out, tuple(state)


def _mixer_call(h, state, params, l, *, tt):
    B, T, D = h.shape
    seg = tt // SUBLANES
    assert T % tt == 0 and tt % SUBLANES == 0 and seg >= POOL_TAIL
    weights = tuple(params[k] for k in _MIXER_WEIGHTS)
    hbm = pl.BlockSpec(memory_space=pl.ANY)
    slab = lambda rows: pltpu.VMEM((N_SLABS, rows, LANES), F32)
    small = pltpu.VMEM((SUBLANES, D_RNN), F32)
    tile_buf = pltpu.VMEM((2, seg, SUBLANES, D), F32)
    return pl.pallas_call(
        functools.partial(_mixer_kernel, tt),
        grid=(B, T // tt),
        in_specs=[hbm] + [_const_spec(a.shape) for a in state] + [_layer_spec(a, l) for a in weights],
        out_specs=hbm,
        out_shape=jax.ShapeDtypeStruct((B, T, D), F32),
        scratch_shapes=[tile_buf, tile_buf, pltpu.SemaphoreType.DMA((2,)), pltpu.SemaphoreType.DMA((2,)),
                        slab((POOL_TAIL + seg) * SUBLANES), slab((CONV_HEAD + seg) * SUBLANES),
                        slab(POOL_TAIL * SUBLANES), slab(CONV_HEAD * SUBLANES),
                        pltpu.VMEM((tt, D_RNN), F32), pltpu.VMEM((tt, D_RNN), F32), pltpu.VMEM((tt, D_RNN), F32),
                        small, small, small, small]
        + _BF16_WEIGHT_SCRATCH,
        compiler_params=pltpu.CompilerParams(dimension_semantics=("arbitrary", "arbitrary"),
                                             vmem_limit_bytes=VMEM_LIMIT_BYTES),
        name="mixer",
    )(h, *state, *weights)


def _mlp_rows(h, g_ref, w_up_ref, w_down_ref):
    xn = _rms(h, g_ref[...]).astype(BF16)
    acc = h
    for c in range(D_FF // FF_CHUNK):
        cols = slice(c * FF_CHUNK, (c + 1) * FF_CHUNK)
        u = jnp.dot(xn, w_up_ref[:, cols].astype(BF16), preferred_element_type=F32)
        u = jnp.maximum(u, 0.0)
        acc = acc + jnp.dot((u * u).astype(BF16), w_down_ref[cols, :].astype(BF16), preferred_element_type=F32)
    return acc


def _mlp_final_kernel(h_ref, g_ref, w_up_ref, w_down_ref, fg_ref, out_ref):
    out_ref[...] = _rms(_mlp_rows(h_ref[...], g_ref, w_up_ref, w_down_ref), fg_ref[...])


def _mlp_with_prefix_kernel(h_ref, hp_ref, g_ref, w_up_ref, w_down_ref, out_ref, outp_ref):
    i = pl.program_id(0)

    @pl.when(i == 0)
    def _():
        n_p = hp_ref.shape[0]
        acc = _mlp_rows(jnp.concatenate([hp_ref[...], h_ref[...]], axis=0), g_ref, w_up_ref, w_down_ref)
        outp_ref[...] = acc[:n_p]
        out_ref[...] = acc[n_p:]

    @pl.when(i > 0)
    def _():
        out_ref[...] = _mlp_rows(h_ref[...], g_ref, w_up_ref, w_down_ref)


def _mlp_call(h, params, l, *, tt, prefix=None):
    shape = h.shape
    rows = h.reshape(-1, D_MODEL)
    n = rows.shape[0]
    assert n % tt == 0
    tile = pl.BlockSpec((tt, D_MODEL), lambda i: (i, 0))
    weights = [params["mlp_g"], params["w_up"], params["w_down"]]
    weight_specs = [_layer_spec(a, l) for a in weights]
    compiler_params = pltpu.CompilerParams(dimension_semantics=("arbitrary",), vmem_limit_bytes=VMEM_LIMIT_BYTES)
    if prefix is None:
        out = pl.pallas_call(
            _mlp_final_kernel,
            grid=(n // tt,),
            in_specs=[tile] + weight_specs + [_const_spec(params["final_g"].shape)],
            out_specs=tile,
            out_shape=jax.ShapeDtypeStruct((n, D_MODEL), F32),
            compiler_params=compiler_params,
            name="mlp_final",
        )(rows, *weights, params["final_g"])
        return out.reshape(shape)
    assert prefix.shape[0] % SUBLANES == 0
    out, prefix_out = pl.pallas_call(
        _mlp_with_prefix_kernel,
        grid=(n // tt,),
        in_specs=[tile, _const_spec(prefix.shape)] + weight_specs,
        out_specs=[tile, pl.BlockSpec(prefix.shape, lambda i: (0, 0))],
        out_shape=[jax.ShapeDtypeStruct((n, D_MODEL), F32), jax.ShapeDtypeStruct(prefix.shape, F32)],
        compiler_params=compiler_params,
        name="mlp",
    )(rows, prefix, *weights)
    return out.reshape(shape), prefix_out


def _block_diag(w, per_tile):
    L, n, d, _ = w.shape
    w = w.reshape(L, n // per_tile, per_tile, d, d)
    eye = jnp.eye(per_tile, dtype=w.dtype)
    bd = w[:, :, :, :, None, :] * eye[None, None, :, None, :, None]
    return bd.reshape(L, n // per_tile, per_tile * d, per_tile * d)


def kernel(x, meta_tokens, mix_norm_g, w_in, pool_w, pool_b, pool_scale, conv_w, conv_b, gate_r_w, gate_r_b,
           gate_i_w, gate_i_b, lru_lambda, group_norm_g, w_out, mlp_norm_g, w_up, w_down, final_norm_g):
    depth = w_in.shape[0]
    rows = lambda v: v.reshape(depth, 1, -1)
    gate_w = jnp.concatenate([_block_diag(gate_r_w, MXU_DIM // RNN_HEAD_DIM),
                              _block_diag(gate_i_w, MXU_DIM // RNN_HEAD_DIM)], axis=-1)
    params = {
        "mix_g": rows(mix_norm_g), "w_in": w_in,
        "pool_w": _block_diag(pool_w, MXU_DIM // POOL_GROUP_DIM).astype(BF16),
        "pool_b": rows(pool_b), "pool_scale": rows(pool_scale), "conv_w": conv_w, "conv_b": rows(conv_b),
        "gate_w": gate_w.astype(BF16), "gate_r_b": rows(gate_r_b), "gate_i_b": rows(gate_i_b),
        "lam": rows(lru_lambda), "gn_g": rows(group_norm_g), "w_out": w_out, "mlp_g": rows(mlp_norm_g),
        "w_up": w_up, "w_down": w_down, "final_g": final_norm_g.reshape(1, -1),
    }

    hm = meta_tokens.astype(x.dtype)
    h = x
    for l in range(depth):
        hm, state = _prefix_mixer_call(hm[None], params, l)
        h = _mixer_call(h, state, params, l, tt=1024)
        if l + 1 < depth:
            h, hm = _mlp_call(h, params, l, tt=512, prefix=hm[0])
        else:
            h = _mlp_call(h, params, l, tt=512)
    return h
```

```python
import functools

import jax
import jax.numpy as jnp
from jax import lax
from jax.experimental import pallas as pl
from jax.experimental.pallas import tpu as pltpu

D_MODEL = 1024
D_POOL = 512
D_RNN = 512
POOL_WINDOWS = (2, 4, 8, 16)
POOL_GROUP_DIM = 128
N_RNN_HEADS = 8
RNN_HEAD_DIM = 64
CONV_WIDTH = 4
LRU_C = 8.0
D_IN_PROJ = D_POOL + 2 * D_RNN
D_FF = 4 * D_MODEL
EPS = 1e-6

SUBLANES = 8
LANES = 128
N_SLABS = D_POOL // LANES
POOL_TAIL = max(POOL_WINDOWS)
CONV_TAIL = SUBLANES
CONV_HEAD = CONV_WIDTH - 1
MXU_DIM = 256
FF_CHUNK = 1024
VMEM_LIMIT_BYTES = 56 * 1024 * 1024

F32 = jnp.float32
BF16 = jnp.bfloat16


def _rms(x, g):
    return x * lax.rsqrt(jnp.mean(x * x, axis=-1, keepdims=True) + EPS) * g


def _rms_unit(x):
    return x * lax.rsqrt(jnp.mean(x * x, axis=-1, keepdims=True) + EPS)


def _cast_weights(g_col_ref, w_in_ref, gng_col_ref, w_out_ref, w_in_bf, w_out_bf):
    w_in_bf[...] = (w_in_ref[...] * g_col_ref[...]).astype(BF16)
    w_out_bf[...] = (w_out_ref[...] * gng_col_ref[...]).astype(BF16)


def _gelu_tanh(x):
    return 0.5 * x * (1.0 + jnp.tanh(0.7978845608028654 * (x + 0.044715 * (x * x * x))))


def _in_proj(h, w_in_bf):
    return jnp.dot(_rms_unit(h).astype(BF16), w_in_bf[...], preferred_element_type=F32)


def _pool_out(pooled, pw_ref, pb_ref, ps_ref):
    mapped = []
    for p in range(D_POOL // MXU_DIM):
        pg = jnp.concatenate(pooled[2 * p:2 * p + 2], axis=-1).astype(BF16)
        mapped.append(jnp.dot(pg, pw_ref[p], preferred_element_type=F32))
    return _rms_unit(jnp.concatenate(mapped, axis=-1) + pb_ref[...] * ps_ref[...])


def _lru_coeffs(xc, gw_ref, rb_ref, ib_ref, lam_ref):
    xcb = xc.astype(BF16)
    g0 = jnp.dot(xcb[:, :MXU_DIM], gw_ref[0], preferred_element_type=F32)
    g1 = jnp.dot(xcb[:, MXU_DIM:], gw_ref[1], preferred_element_type=F32)
    t_r = jnp.tanh(jnp.concatenate([g0[:, :MXU_DIM], g1[:, :MXU_DIM]], axis=-1) + 0.5 * rb_ref[...])
    t_i = jnp.tanh(jnp.concatenate([g0[:, MXU_DIM:], g1[:, MXU_DIM:]], axis=-1) + 0.5 * ib_ref[...])
    lam = lam_ref[...]
    softplus_neg_lam = jnp.maximum(-lam, 0.0) + jnp.log1p(jnp.exp(-jnp.abs(lam)))
    half_c = (-0.5 * LRU_C) * softplus_neg_lam
    log_a = half_c * t_r + half_c
    a = jnp.exp(log_a)
    q = jnp.tanh(-log_a) * (1.0 + a * a)
    b = jnp.where(q > 0.0, q * lax.rsqrt(q), 0.0) * ((0.5 * t_i + 0.5) * xc)
    return a, b


def _out_proj(h, y_pool, h_lru, gate, w_out_bf):
    y_rnn = _rms_unit(h_lru * _gelu_tanh(gate))
    y = jnp.concatenate([y_pool, y_rnn], axis=-1).astype(BF16)
    return h + jnp.dot(y, w_out_bf[...], preferred_element_type=F32)


def _prefix_mixer_kernel(tt,
                         h_ref, g_ref, w_in_ref, pw_ref, pb_ref, ps_ref, cw_ref, cb_ref, gw_ref, rb_ref, ib_ref,
                         lam_ref, gng_ref, w_out_ref,
                         out_ref, ptail_ref, ctail_ref, hst_ref,
                         pool_buf, rnn_buf, w_in_bf, w_out_bf):
    _cast_weights(g_ref, w_in_ref, gng_ref, w_out_ref, w_in_bf, w_out_bf)
    pool_buf[0:POOL_TAIL, :] = jnp.zeros((POOL_TAIL, D_POOL), F32)
    rnn_buf[0:CONV_TAIL, :] = jnp.zeros((CONV_TAIL, D_RNN), F32)

    h = h_ref[0]
    proj = _in_proj(h, w_in_bf)
    pool_buf[POOL_TAIL:POOL_TAIL + tt, :] = proj[:, :D_POOL]
    rnn_buf[CONV_TAIL:CONV_TAIL + tt, :] = proj[:, D_POOL:D_POOL + D_RNN]
    gate = proj[:, D_POOL + D_RNN:]

    frame = lax.broadcasted_iota(jnp.int32, (tt, 1), 0)
    pooled = []
    for g, k in enumerate(POOL_WINDOWS):
        sl = slice(g * POOL_GROUP_DIM, (g + 1) * POOL_GROUP_DIM)
        u = pool_buf[POOL_TAIL:POOL_TAIL + tt, sl]
        s = u
        for j in range(1, k):
            s = s + pool_buf[POOL_TAIL - j:POOL_TAIL - j + tt, sl]
        pooled.append(s * (1.0 / jnp.minimum(frame + 1, k).astype(F32)) - u)
    y_pool = _pool_out(pooled, pw_ref, pb_ref, ps_ref)

    xc = cb_ref[...]
    for k in range(CONV_WIDTH):
        off = CONV_TAIL - CONV_HEAD + k
        xc = xc + rnn_buf[off:off + tt, :] * cw_ref[k:k + 1, :]
    a, b = _lru_coeffs(xc, gw_ref, rb_ref, ib_ref, lam_ref)
    state = jnp.zeros((1, D_RNN), F32)
    h_rows = []
    for j in range(tt):
        state = a[j:j + 1, :] * state + b[j:j + 1, :]
        h_rows.append(state)
    h_lru = jnp.concatenate(h_rows, axis=0)

    out_ref[0] = _out_proj(h, y_pool, h_lru, gate, w_out_bf)
    ptail_ref[...] = pool_buf[tt:tt + POOL_TAIL, :]
    ctail_ref[...] = rnn_buf[tt:tt + CONV_TAIL, :]
    hst_ref[...] = jnp.broadcast_to(state, (SUBLANES, D_RNN))


def _fill_heads(e_ref, tb_ref, n_head, seg, sub):
    for g in range(N_SLABS):
        for v in range(n_head):
            cur = e_ref[g, (seg + v) * SUBLANES:(seg + v + 1) * SUBLANES, :]
            prev = tb_ref[g, v * SUBLANES:(v + 1) * SUBLANES, :]
            e_ref[g, v * SUBLANES:(v + 1) * SUBLANES, :] = pltpu.roll(
                jnp.where(sub == SUBLANES - 1, prev, cur), 1, 0)
        tb_ref[g] = e_ref[g, seg * SUBLANES:(seg + n_head) * SUBLANES, :]


def _tile_copies(hbm, buf, sem, b, t, slot, tt, to_vmem):
    seg = tt // SUBLANES
    copies = []
    for s in range(SUBLANES):
        rows = hbm.at[b, pl.ds(t * tt + s * seg, seg), :]
        regs = buf.at[slot, :, s, :]
        copies.append(pltpu.make_async_copy(rows, regs, sem.at[slot]) if to_vmem
                      else pltpu.make_async_copy(regs, rows, sem.at[slot]))
    return copies


def _mixer_kernel(tt,
                  h_hbm, ptail0_ref, ctail0_ref, hst0_ref,
                  g_ref, w_in_ref, pw_ref, pb_ref, ps_ref, cw_ref, cb_ref, gw_ref, rb_ref, ib_ref,
                  lam_ref, gng_ref, w_out_ref,
                  out_hbm,
                  in_buf, out_buf, in_sem, out_sem,
                  pool_e, rnn_e, pool_tb, rnn_tb, hl_buf, ac_buf, hs_buf,
                  aend_ref, hend_ref, carry_ref, st_ref, w_in_bf, w_out_bf):
    seg = tt // SUBLANES
    b = pl.program_id(0)
    t = pl.program_id(1)
    n_t = pl.num_programs(1)
    n = b * n_t + t
    n_steps = pl.num_programs(0) * n_t
    slot = n % 2

    @pl.when(n == 0)
    def _():
        for c in _tile_copies(h_hbm, in_buf, in_sem, b, t, slot, tt, True):
            c.start()

    @pl.when(n + 1 < n_steps)
    def _():
        wrap = t + 1 == n_t
        for c in _tile_copies(h_hbm, in_buf, in_sem, jnp.where(wrap, b + 1, b), jnp.where(wrap, 0, t + 1),
                              1 - slot, tt, True):
            c.start()

    @pl.when(t == 0)
    def _():
        _cast_weights(g_ref, w_in_ref, gng_ref, w_out_ref, w_in_bf, w_out_bf)
        for g in range(N_SLABS):
            lanes = slice(g * LANES, (g + 1) * LANES)
            for v in range(POOL_TAIL):
                pool_tb[g, v * SUBLANES:(v + 1) * SUBLANES, :] = jnp.broadcast_to(
                    ptail0_ref[v:v + 1, lanes], (SUBLANES, LANES))
            for v in range(CONV_HEAD):
                row = CONV_TAIL - CONV_HEAD + v
                rnn_tb[g, v * SUBLANES:(v + 1) * SUBLANES, :] = jnp.broadcast_to(
                    ctail0_ref[row:row + 1, lanes], (SUBLANES, LANES))
        st_ref[...] = hst0_ref[...]

    for c in _tile_copies(h_hbm, in_buf, in_sem, b, t, slot, tt, True):
        c.wait()
    h = in_buf[slot].reshape(tt, D_MODEL)
    xn = _rms_unit(h).astype(BF16)
    sub = lax.broadcasted_iota(jnp.int32, (SUBLANES, LANES), 0)
    u_rnn = jnp.dot(xn, w_in_bf[:, D_POOL:D_POOL + D_RNN], preferred_element_type=F32)
    for g in range(N_SLABS):
        rnn_e[g, CONV_HEAD * SUBLANES:, :] = u_rnn[:, g * LANES:(g + 1) * LANES]
    _fill_heads(rnn_e, rnn_tb, CONV_HEAD, seg, sub)
    xc = []
    for g in range(N_SLABS):
        lanes = slice(g * LANES, (g + 1) * LANES)
        acc = cb_ref[:, lanes]
        for k in range(CONV_WIDTH):
            acc = acc + rnn_e[g, k * SUBLANES:k * SUBLANES + tt, :] * cw_ref[k:k + 1, lanes]
        xc.append(acc)
    xc = jnp.concatenate(xc, axis=-1)

    u_pool = jnp.dot(xn, w_in_bf[:, :D_POOL], preferred_element_type=F32)
    for g in range(N_SLABS):
        pool_e[g, POOL_TAIL * SUBLANES:, :] = u_pool[:, g * LANES:(g + 1) * LANES]
    _fill_heads(pool_e, pool_tb, POOL_TAIL, seg, sub)

    pooled = []
    for g, k in enumerate(POOL_WINDOWS):
        e = pool_e[g]
        w = e
        span = 1
        while span < k:
            w = w[span * SUBLANES:] + w[:-span * SUBLANES]
            span *= 2
        pooled.append(w[-tt:] * (1.0 / k) - e[POOL_TAIL * SUBLANES:])
    a, b_in = _lru_coeffs(xc, gw_ref, rb_ref, ib_ref, lam_ref)
    y_pool = _pool_out(pooled, pw_ref, pb_ref, ps_ref)
    gate = jnp.dot(xn, w_in_bf[:, D_POOL + D_RNN:], preferred_element_type=F32)
    res_pool = jnp.dot(y_pool.astype(BF16), w_out_bf[:D_POOL, :], preferred_element_type=F32)

    hl = ac = None
    for j in range(seg):
        rows = slice(j * SUBLANES, (j + 1) * SUBLANES)
        hl = b_in[rows] if j == 0 else a[rows] * hl + b_in[rows]
        ac = a[rows] if j == 0 else a[rows] * ac
        hl_buf[rows, :] = hl
        ac_buf[rows, :] = ac
    aend_ref[...] = ac
    hend_ref[...] = hl
    c = st_ref[0:1, :]
    for s in range(SUBLANES):
        carry_ref[s:s + 1, :] = c
        c = aend_ref[s:s + 1, :] * c + hend_ref[s:s + 1, :]
    st_ref[0:1, :] = c
    carry = carry_ref[...]
    for j in range(seg):
        rows = slice(j * SUBLANES, (j + 1) * SUBLANES)
        hs_buf[rows, :] = hl_buf[rows, :] + ac_buf[rows, :] * carry

    y_rnn = _rms_unit(hs_buf[...] * _gelu_tanh(gate)).astype(BF16)
    res = h + res_pool + jnp.dot(y_rnn, w_out_bf[D_POOL:, :], preferred_element_type=F32)

    @pl.when(n >= 2)
    def _():
        for c in _tile_copies(out_hbm, out_buf, out_sem, b, t, slot, tt, False):
            c.wait()

    out_buf[slot] = res.reshape(seg, SUBLANES, D_MODEL)
    for c in _tile_copies(out_hbm, out_buf, out_sem, b, t, slot, tt, False):
        c.start()

    @pl.when(n == n_steps - 1)
    def _():
        for c in _tile_copies(out_hbm, out_buf, out_sem, b, t, slot, tt, False):
            c.wait()

        @pl.when(n >= 1)
        def _():
            for c in _tile_copies(out_hbm, out_buf, out_sem, b, t, 1 - slot, tt, False):
                c.wait()


def _const_spec(shape):
    return pl.BlockSpec(shape, lambda *_: (0,) * len(shape), pipeline_mode=pl.Buffered(1))


def _layer_spec(arr, l):
    rest = arr.shape[1:]
    return pl.BlockSpec((None,) + rest, lambda *_: (l,) + (0,) * len(rest), pipeline_mode=pl.Buffered(1))


_MIXER_WEIGHTS = ("mix_g", "w_in", "pool_w", "pool_b", "pool_scale", "conv_w", "conv_b",
                  "gate_w", "gate_r_b", "gate_i_b", "lam", "gn_g", "w_out")
_BF16_WEIGHT_SCRATCH = [pltpu.VMEM((D_MODEL, D_IN_PROJ), BF16), pltpu.VMEM((D_MODEL, D_MODEL), BF16)]


def _prefix_mixer_call(h, params, l):
    _, tt, D = h.shape
    weights = tuple(params[k] for k in _MIXER_WEIGHTS)
    state_shapes = ((POOL_TAIL, D_POOL), (CONV_TAIL, D_RNN), (SUBLANES, D_RNN))
    out, *state = pl.pallas_call(
        functools.partial(_prefix_mixer_kernel, tt),
        grid=(1,),
        in_specs=[_const_spec(h.shape)] + [_layer_spec(a, l) for a in weights],
        out_specs=[pl.BlockSpec(s, lambda i, n=len(s): (0,) * n) for s in (h.shape,) + state_shapes],
        out_shape=[jax.ShapeDtypeStruct(h.shape, F32)] + [jax.ShapeDtypeStruct(s, F32) for s in state_shapes],
        scratch_shapes=[pltpu.VMEM((POOL_TAIL + tt, D_POOL), F32),
                        pltpu.VMEM((CONV_TAIL + tt, D_RNN), F32)] + _BF16_WEIGHT_SCRATCH,
        compiler_params=pltpu.CompilerParams(dimension_semantics=("arbitrary",),
                                             vmem_limit_bytes=VMEM_LIMIT_BYTES),
        name="prefix_mixer",
    )(h, *weights)
    return out, tuple(state)


def _mixer_call(h, state, params, l, *, tt):
    B, T, D = h.shape
    seg = tt // SUBLANES
    assert T % tt == 0 and tt % SUBLANES == 0 and seg >= POOL_TAIL
    weights = tuple(params[k] for k in _MIXER_WEIGHTS)
    hbm = pl.BlockSpec(memory_space=pl.ANY)
    slab = lambda rows: pltpu.VMEM((N_SLABS, rows, LANES), F32)
    small = pltpu.VMEM((SUBLANES, D_RNN), F32)
    tile_buf = pltpu.VMEM((2, seg, SUBLANES, D), F32)
    return pl.pallas_call(
        functools.partial(_mixer_kernel, tt),
        grid=(B, T // tt),
        in_specs=[hbm] + [_const_spec(a.shape) for a in state] + [_layer_spec(a, l) for a in weights],
        out_specs=hbm,
        out_shape=jax.ShapeDtypeStruct((B, T, D), F32),
        scratch_shapes=[tile_buf, tile_buf, pltpu.SemaphoreType.DMA((2,)), pltpu.SemaphoreType.DMA((2,)),
                        slab((POOL_TAIL + seg) * SUBLANES), slab((CONV_HEAD + seg) * SUBLANES),
                        slab(POOL_TAIL * SUBLANES), slab(CONV_HEAD * SUBLANES),
                        pltpu.VMEM((tt, D_RNN), F32), pltpu.VMEM((tt, D_RNN), F32), pltpu.VMEM((tt, D_RNN), F32),
                        small, small, small, small]
        + _BF16_WEIGHT_SCRATCH,
        compiler_params=pltpu.CompilerParams(dimension_semantics=("arbitrary", "arbitrary"),
                                             vmem_limit_bytes=VMEM_LIMIT_BYTES),
        name="mixer",
    )(h, *state, *weights)


def _mlp_rows(h, g_ref, w_up_ref, w_down_ref):
    xn = _rms(h, g_ref[...]).astype(BF16)
    acc = h
    for c in range(D_FF // FF_CHUNK):
        cols = slice(c * FF_CHUNK, (c + 1) * FF_CHUNK)
        u = jnp.dot(xn, w_up_ref[:, cols].astype(BF16), preferred_element_type=F32)
        u = jnp.maximum(u, 0.0)
        acc = acc + jnp.dot((u * u).astype(BF16), w_down_ref[cols, :].astype(BF16), preferred_element_type=F32)
    return acc


def _mlp_final_kernel(h_ref, g_ref, w_up_ref, w_down_ref, fg_ref, out_ref):
    out_ref[...] = _rms(_mlp_rows(h_ref[...], g_ref, w_up_ref, w_down_ref), fg_ref[...])


def _mlp_with_prefix_kernel(h_ref, hp_ref, g_ref, w_up_ref, w_down_ref, out_ref, outp_ref):
    i = pl.program_id(0)

    @pl.when(i == 0)
    def _():
        n_p = hp_ref.shape[0]
        acc = _mlp_rows(jnp.concatenate([hp_ref[...], h_ref[...]], axis=0), g_ref, w_up_ref, w_down_ref)
        outp_ref[...] = acc[:n_p]
        out_ref[...] = acc[n_p:]

    @pl.when(i > 0)
    def _():
        out_ref[...] = _mlp_rows(h_ref[...], g_ref, w_up_ref, w_down_ref)


def _mlp_call(h, params, l, *, tt, prefix=None):
    shape = h.shape
    rows = h.reshape(-1, D_MODEL)
    n = rows.shape[0]
    assert n % tt == 0
    tile = pl.BlockSpec((tt, D_MODEL), lambda i: (i, 0))
    weights = [params["mlp_g"], params["w_up"], params["w_down"]]
    weight_specs = [_layer_spec(a, l) for a in weights]
    compiler_params = pltpu.CompilerParams(dimension_semantics=("arbitrary",), vmem_limit_bytes=VMEM_LIMIT_BYTES)
    if prefix is None:
        out = pl.pallas_call(
            _mlp_final_kernel,
            grid=(n // tt,),
            in_specs=[tile] + weight_specs + [_const_spec(params["final_g"].shape)],
            out_specs=tile,
            out_shape=jax.ShapeDtypeStruct((n, D_MODEL), F32),
            compiler_params=compiler_params,
            name="mlp_final",
        )(rows, *weights, params["final_g"])
        return out.reshape(shape)
    assert prefix.shape[0] % SUBLANES == 0
    out, prefix_out = pl.pallas_call(
        _mlp_with_prefix_kernel,
        grid=(n // tt,),
        in_specs=[tile, _const_spec(prefix.shape)] + weight_specs,
        out_specs=[tile, pl.BlockSpec(prefix.shape, lambda i: (0, 0))],
        out_shape=[jax.ShapeDtypeStruct((n, D_MODEL), F32), jax.ShapeDtypeStruct(prefix.shape, F32)],
        compiler_params=compiler_params,
        name="mlp",
    )(rows, prefix, *weights)
    return out.reshape(shape), prefix_out


def _block_diag(w, per_tile):
    L, n, d, _ = w.shape
    w = w.reshape(L, n // per_tile, per_tile, d, d)
    eye = jnp.eye(per_tile, dtype=w.dtype)
    bd = w[:, :, :, :, None, :] * eye[None, None, :, None, :, None]
    return bd.reshape(L, n // per_tile, per_tile * d, per_tile * d)


def kernel(x, meta_tokens, mix_norm_g, w_in, pool_w, pool_b, pool_scale, conv_w, conv_b, gate_r_w, gate_r_b,
           gate_i_w, gate_i_b, lru_lambda, group_norm_g, w_out, mlp_norm_g, w_up, w_down, final_norm_g):
    depth = w_in.shape[0]
    rows = lambda v: v.reshape(depth, 1, -1)
    cols = lambda v: v.reshape(depth, -1, 1)
    gate_w = 0.5 * jnp.concatenate([_block_diag(gate_r_w, MXU_DIM // RNN_HEAD_DIM),
                                    _block_diag(gate_i_w, MXU_DIM // RNN_HEAD_DIM)], axis=-1)
    pool_tiles = (_block_diag(pool_w, MXU_DIM // POOL_GROUP_DIM)
                  * pool_scale.reshape(depth, D_POOL // MXU_DIM, 1, MXU_DIM))
    params = {
        "mix_g": cols(mix_norm_g), "w_in": w_in,
        "pool_w": pool_tiles.astype(BF16),
        "pool_b": rows(pool_b), "pool_scale": rows(pool_scale), "conv_w": conv_w, "conv_b": rows(conv_b),
        "gate_w": gate_w.astype(BF16), "gate_r_b": rows(gate_r_b), "gate_i_b": rows(gate_i_b),
        "lam": rows(lru_lambda), "gn_g": cols(group_norm_g), "w_out": w_out, "mlp_g": rows(mlp_norm_g),
        "w_up": w_up, "w_down": w_down, "final_g": final_norm_g.reshape(1, -1),
    }

    hm = meta_tokens.astype(x.dtype)
    h = x
    for l in range(depth):
        hm, state = _prefix_mixer_call(hm[None], params, l)
        h = _mixer_call(h, state, params, l, tt=1024)
        if l + 1 < depth:
            h, hm = _mlp_call(h, params, l, tt=512, prefix=hm[0])
        else:
            h = _mlp_call(h, params, l, tt=512)
    return h
```

```python
import functools

import jax
import jax.numpy as jnp
from jax import lax
from jax.experimental import pallas as pl
from jax.experimental.pallas import tpu as pltpu

D_MODEL = 1024
D_POOL = 512
D_RNN = 512
POOL_WINDOWS = (2, 4, 8, 16)
POOL_GROUP_DIM = 128
N_RNN_HEADS = 8
RNN_HEAD_DIM = 64
CONV_WIDTH = 4
LRU_C = 8.0
D_IN_PROJ = D_POOL + 2 * D_RNN
D_FF = 4 * D_MODEL
EPS = 1e-6

SUBLANES = 8
LANES = 128
N_SLABS = D_POOL // LANES
POOL_TAIL = max(POOL_WINDOWS)
CONV_TAIL = SUBLANES
CONV_HEAD = CONV_WIDTH - 1
MXU_DIM = 256
FF_CHUNK = 1024
VMEM_LIMIT_BYTES = 56 * 1024 * 1024

F32 = jnp.float32
BF16 = jnp.bfloat16


def _rms(x, g):
    return x * lax.rsqrt(jnp.mean(x * x, axis=-1, keepdims=True) + EPS) * g


def _rms_unit(x):
    return x * lax.rsqrt(jnp.mean(x * x, axis=-1, keepdims=True) + EPS)


def _cast_weights(gains_ref, w_in_ref, w_out_ref, w_in_bf, w_out_bf):
    gains = gains_ref[...]
    w_in_bf[...] = (w_in_ref[...] * gains[:, 0:1]).astype(BF16)
    w_out_bf[...] = (w_out_ref[...] * gains[:, 1:2]).astype(BF16)


def _row_refs(rows_ref):
    return [rows_ref.at[pl.ds(i, 1)] for i in range(len(_ROW_PARAMS))]


def _gelu_tanh(x):
    half_x = 0.5 * x
    return half_x + half_x * jnp.tanh(x * (0.7978845608028654 + (0.7978845608028654 * 0.044715) * (x * x)))


def _in_proj(h, w_in_bf):
    return jnp.dot(_rms_unit(h).astype(BF16), w_in_bf[...], preferred_element_type=F32)


def _pool_out(pooled, pw_ref, pb_ref, ps_ref):
    mapped = []
    for p in range(D_POOL // MXU_DIM):
        pg = jnp.concatenate(pooled[2 * p:2 * p + 2], axis=-1).astype(BF16)
        mapped.append(jnp.dot(pg, pw_ref[p], preferred_element_type=F32))
    return _rms_unit(jnp.concatenate(mapped, axis=-1) + pb_ref[...] * ps_ref[...])


def _lru_coeffs(xc, gw_ref, rb_ref, ib_ref, lam_ref):
    xcb = xc.astype(BF16)
    g0 = jnp.dot(xcb[:, :MXU_DIM], gw_ref[0], preferred_element_type=F32)
    g1 = jnp.dot(xcb[:, MXU_DIM:], gw_ref[1], preferred_element_type=F32)
    t_r = jnp.tanh(jnp.concatenate([g0[:, :MXU_DIM], g1[:, :MXU_DIM]], axis=-1) + 0.5 * rb_ref[...])
    t_i = jnp.tanh(jnp.concatenate([g0[:, MXU_DIM:], g1[:, MXU_DIM:]], axis=-1) + 0.5 * ib_ref[...])
    lam = lam_ref[...]
    softplus_neg_lam = jnp.maximum(-lam, 0.0) + jnp.log1p(jnp.exp(-jnp.abs(lam)))
    half_c = (-0.5 * LRU_C) * softplus_neg_lam
    log_a = half_c * t_r + half_c
    a = jnp.exp(log_a)
    q = jnp.tanh(-log_a) * (1.0 + a * a)
    b = jnp.where(q > 0.0, q * lax.rsqrt(q), 0.0) * ((0.5 * t_i + 0.5) * xc)
    return a, b


def _out_proj(h, y_pool, h_lru, gate, w_out_bf):
    y_rnn = _rms_unit(h_lru * _gelu_tanh(gate))
    y = jnp.concatenate([y_pool, y_rnn], axis=-1).astype(BF16)
    return h + jnp.dot(y, w_out_bf[...], preferred_element_type=F32)


def _prefix_mixer_kernel(tt,
                         h_ref, gains_ref, w_in_ref, pw_ref, rows_ref, cw_ref, gw_ref, w_out_ref,
                         out_ref, ptail_ref, ctail_ref, hst_ref,
                         pool_buf, rnn_buf, w_in_bf, w_out_bf):
    pb_ref, ps_ref, cb_ref, rb_ref, ib_ref, lam_ref = _row_refs(rows_ref)
    _cast_weights(gains_ref, w_in_ref, w_out_ref, w_in_bf, w_out_bf)
    pool_buf[0:POOL_TAIL, :] = jnp.zeros((POOL_TAIL, D_POOL), F32)
    rnn_buf[0:CONV_TAIL, :] = jnp.zeros((CONV_TAIL, D_RNN), F32)

    h = h_ref[0]
    proj = _in_proj(h, w_in_bf)
    pool_buf[POOL_TAIL:POOL_TAIL + tt, :] = proj[:, :D_POOL]
    rnn_buf[CONV_TAIL:CONV_TAIL + tt, :] = proj[:, D_POOL:D_POOL + D_RNN]
    gate = proj[:, D_POOL + D_RNN:]

    frame = lax.broadcasted_iota(jnp.int32, (tt, 1), 0)
    pooled = []
    for g, k in enumerate(POOL_WINDOWS):
        sl = slice(g * POOL_GROUP_DIM, (g + 1) * POOL_GROUP_DIM)
        u = pool_buf[POOL_TAIL:POOL_TAIL + tt, sl]
        s = u
        for j in range(1, k):
            s = s + pool_buf[POOL_TAIL - j:POOL_TAIL - j + tt, sl]
        pooled.append(s * (1.0 / jnp.minimum(frame + 1, k).astype(F32)) - u)
    y_pool = _pool_out(pooled, pw_ref, pb_ref, ps_ref)

    xc = cb_ref[...]
    for k in range(CONV_WIDTH):
        off = CONV_TAIL - CONV_HEAD + k
        xc = xc + rnn_buf[off:off + tt, :] * cw_ref[k:k + 1, :]
    a, b = _lru_coeffs(xc, gw_ref, rb_ref, ib_ref, lam_ref)
    state = jnp.zeros((1, D_RNN), F32)
    h_rows = []
    for j in range(tt):
        state = a[j:j + 1, :] * state + b[j:j + 1, :]
        h_rows.append(state)
    h_lru = jnp.concatenate(h_rows, axis=0)

    out_ref[0] = _out_proj(h, y_pool, h_lru, gate, w_out_bf)
    ptail_ref[...] = pool_buf[tt:tt + POOL_TAIL, :]
    ctail_ref[...] = rnn_buf[tt:tt + CONV_TAIL, :]
    hst_ref[...] = jnp.broadcast_to(state, (SUBLANES, D_RNN))


def _fill_heads(e_ref, tb_ref, n_head, seg, sub):
    for g in range(N_SLABS):
        for v in range(n_head):
            cur = e_ref[g, (seg + v) * SUBLANES:(seg + v + 1) * SUBLANES, :]
            prev = tb_ref[g, v * SUBLANES:(v + 1) * SUBLANES, :]
            e_ref[g, v * SUBLANES:(v + 1) * SUBLANES, :] = pltpu.roll(
                jnp.where(sub == SUBLANES - 1, prev, cur), 1, 0)
        tb_ref[g] = e_ref[g, seg * SUBLANES:(seg + n_head) * SUBLANES, :]


def _tile_copies(hbm, buf, sem, b, t, slot, tt, to_vmem):
    seg = tt // SUBLANES
    copies = []
    for s in range(SUBLANES):
        rows = hbm.at[b, pl.ds(t * tt + s * seg, seg), :]
        regs = buf.at[slot, :, s, :]
        copies.append(pltpu.make_async_copy(rows, regs, sem.at[slot]) if to_vmem
                      else pltpu.make_async_copy(regs, rows, sem.at[slot]))
    return copies


def _mixer_kernel(tt,
                  h_hbm, ptail0_ref, ctail0_ref, hst0_ref,
                  gains_ref, w_in_ref, pw_ref, rows_ref, cw_ref, gw_ref, w_out_ref,
                  out_hbm,
                  in_buf, out_buf, in_sem, out_sem,
                  pool_e, rnn_e, pool_tb, rnn_tb, hl_buf, ac_buf, hs_buf,
                  aend_ref, hend_ref, carry_ref, st_ref, w_in_bf, w_out_bf):
    seg = tt // SUBLANES
    b = pl.program_id(0)
    t = pl.program_id(1)
    n_t = pl.num_programs(1)
    n = b * n_t + t
    n_steps = pl.num_programs(0) * n_t
    slot = n % 2
    pb_ref, ps_ref, cb_ref, rb_ref, ib_ref, lam_ref = _row_refs(rows_ref)

    @pl.when(n == 0)
    def _():
        for c in _tile_copies(h_hbm, in_buf, in_sem, b, t, slot, tt, True):
            c.start()
        _cast_weights(gains_ref, w_in_ref, w_out_ref, w_in_bf, w_out_bf)

    @pl.when(n + 1 < n_steps)
    def _():
        wrap = t + 1 == n_t
        for c in _tile_copies(h_hbm, in_buf, in_sem, jnp.where(wrap, b + 1, b), jnp.where(wrap, 0, t + 1),
                              1 - slot, tt, True):
            c.start()

    @pl.when(t == 0)
    def _():
        for g in range(N_SLABS):
            lanes = slice(g * LANES, (g + 1) * LANES)
            for v in range(POOL_TAIL):
                pool_tb[g, v * SUBLANES:(v + 1) * SUBLANES, :] = jnp.broadcast_to(
                    ptail0_ref[v:v + 1, lanes], (SUBLANES, LANES))
            for v in range(CONV_HEAD):
                row = CONV_TAIL - CONV_HEAD + v
                rnn_tb[g, v * SUBLANES:(v + 1) * SUBLANES, :] = jnp.broadcast_to(
                    ctail0_ref[row:row + 1, lanes], (SUBLANES, LANES))
        st_ref[...] = hst0_ref[...]

    for c in _tile_copies(h_hbm, in_buf, in_sem, b, t, slot, tt, True):
        c.wait()
    h = in_buf[slot].reshape(tt, D_MODEL)
    xn = _rms_unit(h).astype(BF16)
    sub = lax.broadcasted_iota(jnp.int32, (SUBLANES, LANES), 0)
    u_rnn = jnp.dot(xn, w_in_bf[:, D_POOL:D_POOL + D_RNN], preferred_element_type=F32)
    for g in range(N_SLABS):
        rnn_e[g, CONV_HEAD * SUBLANES:, :] = u_rnn[:, g * LANES:(g + 1) * LANES]
    _fill_heads(rnn_e, rnn_tb, CONV_HEAD, seg, sub)
    xc = []
    for g in range(N_SLABS):
        lanes = slice(g * LANES, (g + 1) * LANES)
        acc = cb_ref[:, lanes]
        for k in range(CONV_WIDTH):
            acc = acc + rnn_e[g, k * SUBLANES:k * SUBLANES + tt, :] * cw_ref[k:k + 1, lanes]
        xc.append(acc)
    xc = jnp.concatenate(xc, axis=-1)

    u_pool = jnp.dot(xn, w_in_bf[:, :D_POOL], preferred_element_type=F32)
    for g in range(N_SLABS):
        pool_e[g, POOL_TAIL * SUBLANES:, :] = u_pool[:, g * LANES:(g + 1) * LANES]
    _fill_heads(pool_e, pool_tb, POOL_TAIL, seg, sub)

    pooled = []
    for g, k in enumerate(POOL_WINDOWS):
        e = pool_e[g]
        w = e
        span = 1
        while span < k:
            w = w[span * SUBLANES:] + w[:-span * SUBLANES]
            span *= 2
        pooled.append(w[-tt:] * (1.0 / k) - e[POOL_TAIL * SUBLANES:])
    a, b_in = _lru_coeffs(xc, gw_ref, rb_ref, ib_ref, lam_ref)
    y_pool = _pool_out(pooled, pw_ref, pb_ref, ps_ref)
    gate = jnp.dot(xn, w_in_bf[:, D_POOL + D_RNN:], preferred_element_type=F32)
    res_pool = jnp.dot(y_pool.astype(BF16), w_out_bf[:D_POOL, :], preferred_element_type=F32)

    hl = ac = None
    for j in range(seg):
        rows = slice(j * SUBLANES, (j + 1) * SUBLANES)
        hl = b_in[rows] if j == 0 else a[rows] * hl + b_in[rows]
        ac = a[rows] if j == 0 else a[rows] * ac
        hl_buf[rows, :] = hl
        ac_buf[rows, :] = ac
    aend_ref[...] = ac
    hend_ref[...] = hl
    c = st_ref[0:1, :]
    for s in range(SUBLANES):
        carry_ref[s:s + 1, :] = c
        c = aend_ref[s:s + 1, :] * c + hend_ref[s:s + 1, :]
    st_ref[0:1, :] = c
    carry = carry_ref[...]
    for j in range(seg):
        rows = slice(j * SUBLANES, (j + 1) * SUBLANES)
        hs_buf[rows, :] = hl_buf[rows, :] + ac_buf[rows, :] * carry

    y_rnn = _rms_unit(hs_buf[...] * _gelu_tanh(gate)).astype(BF16)
    res = h + res_pool + jnp.dot(y_rnn, w_out_bf[D_POOL:, :], preferred_element_type=F32)

    @pl.when(n >= 2)
    def _():
        for c in _tile_copies(out_hbm, out_buf, out_sem, b, t, slot, tt, False):
            c.wait()

    out_buf[slot] = res.reshape(seg, SUBLANES, D_MODEL)
    for c in _tile_copies(out_hbm, out_buf, out_sem, b, t, slot, tt, False):
        c.start()

    @pl.when(n == n_steps - 1)
    def _():
        for c in _tile_copies(out_hbm, out_buf, out_sem, b, t, slot, tt, False):
            c.wait()

        @pl.when(n >= 1)
        def _():
            for c in _tile_copies(out_hbm, out_buf, out_sem, b, t, 1 - slot, tt, False):
                c.wait()


def _const_spec(shape):
    return pl.BlockSpec(shape, lambda *_: (0,) * len(shape), pipeline_mode=pl.Buffered(1))


def _layer_spec(arr, l):
    rest = arr.shape[1:]
    return pl.BlockSpec((None,) + rest, lambda *_: (l,) + (0,) * len(rest), pipeline_mode=pl.Buffered(1))


_MIXER_WEIGHTS = ("gains", "w_in", "pool_w", "rows", "conv_w", "gate_w", "w_out")
_ROW_PARAMS = ("pool_b", "pool_scale", "conv_b", "gate_r_b", "gate_i_b", "lam")
_BF16_WEIGHT_SCRATCH = [pltpu.VMEM((D_MODEL, D_IN_PROJ), BF16), pltpu.VMEM((D_MODEL, D_MODEL), BF16)]


def _prefix_mixer_call(h, params, l):
    _, tt, D = h.shape
    weights = tuple(params[k] for k in _MIXER_WEIGHTS)
    state_shapes = ((POOL_TAIL, D_POOL), (CONV_TAIL, D_RNN), (SUBLANES, D_RNN))
    out, *state = pl.pallas_call(
        functools.partial(_prefix_mixer_kernel, tt),
        grid=(1,),
        in_specs=[_const_spec(h.shape)] + [_layer_spec(a, l) for a in weights],
        out_specs=[pl.BlockSpec(s, lambda i, n=len(s): (0,) * n) for s in (h.shape,) + state_shapes],
        out_shape=[jax.ShapeDtypeStruct(h.shape, F32)] + [jax.ShapeDtypeStruct(s, F32) for s in state_shapes],
        scratch_shapes=[pltpu.VMEM((POOL_TAIL + tt, D_POOL), F32),
                        pltpu.VMEM((CONV_TAIL + tt, D_RNN), F32)] + _BF16_WEIGHT_SCRATCH,
        compiler_params=pltpu.CompilerParams(dimension_semantics=("arbitrary",),
                                             vmem_limit_bytes=VMEM_LIMIT_BYTES),
        name="prefix_mixer",
    )(h, *weights)
    return out, tuple(state)


def _mixer_call(h, state, params, l, *, tt):
    B, T, D = h.shape
    seg = tt // SUBLANES
    assert T % tt == 0 and tt % SUBLANES == 0 and seg >= POOL_TAIL
    weights = tuple(params[k] for k in _MIXER_WEIGHTS)
    hbm = pl.BlockSpec(memory_space=pl.ANY)
    slab = lambda rows: pltpu.VMEM((N_SLABS, rows, LANES), F32)
    small = pltpu.VMEM((SUBLANES, D_RNN), F32)
    tile_buf = pltpu.VMEM((2, seg, SUBLANES, D), F32)
    return pl.pallas_call(
        functools.partial(_mixer_kernel, tt),
        grid=(B, T // tt),
        in_specs=[hbm] + [_const_spec(a.shape) for a in state] + [_layer_spec(a, l) for a in weights],
        out_specs=hbm,
        out_shape=jax.ShapeDtypeStruct((B, T, D), F32),
        scratch_shapes=[tile_buf, tile_buf, pltpu.SemaphoreType.DMA((2,)), pltpu.SemaphoreType.DMA((2,)),
                        slab((POOL_TAIL + seg) * SUBLANES), slab((CONV_HEAD + seg) * SUBLANES),
                        slab(POOL_TAIL * SUBLANES), slab(CONV_HEAD * SUBLANES),
                        pltpu.VMEM((tt, D_RNN), F32), pltpu.VMEM((tt, D_RNN), F32), pltpu.VMEM((tt, D_RNN), F32),
                        small, small, small, small]
        + _BF16_WEIGHT_SCRATCH,
        compiler_params=pltpu.CompilerParams(dimension_semantics=("arbitrary", "arbitrary"),
                                             vmem_limit_bytes=VMEM_LIMIT_BYTES),
        name="mixer",
    )(h, *state, *weights)


def _mlp_rows(h, g_ref, w_up_ref, w_down_ref):
    xn = _rms(h, g_ref[...]).astype(BF16)
    acc = h
    for c in range(D_FF // FF_CHUNK):
        cols = slice(c * FF_CHUNK, (c + 1) * FF_CHUNK)
        u = jnp.dot(xn, w_up_ref[:, cols].astype(BF16), preferred_element_type=F32)
        u = jnp.maximum(u, 0.0)
        acc = acc + jnp.dot((u * u).astype(BF16), w_down_ref[cols, :].astype(BF16), preferred_element_type=F32)
    return acc


def _mlp_final_kernel(h_ref, g_ref, w_up_ref, w_down_ref, fg_ref, out_ref):
    out_ref[...] = _rms(_mlp_rows(h_ref[...], g_ref, w_up_ref, w_down_ref), fg_ref[...])


def _mlp_with_prefix_kernel(h_ref, hp_ref, g_ref, w_up_ref, w_down_ref, out_ref, outp_ref):
    i = pl.program_id(0)

    @pl.when(i == 0)
    def _():
        n_p = hp_ref.shape[0]
        acc = _mlp_rows(jnp.concatenate([hp_ref[...], h_ref[...]], axis=0), g_ref, w_up_ref, w_down_ref)
        outp_ref[...] = acc[:n_p]
        out_ref[...] = acc[n_p:]

    @pl.when(i > 0)
    def _():
        out_ref[...] = _mlp_rows(h_ref[...], g_ref, w_up_ref, w_down_ref)


def _mlp_call(h, params, l, *, tt, prefix=None):
    shape = h.shape
    rows = h.reshape(-1, D_MODEL)
    n = rows.shape[0]
    assert n % tt == 0
    tile = pl.BlockSpec((tt, D_MODEL), lambda i: (i, 0))
    weights = [params["mlp_g"], params["w_up"], params["w_down"]]
    weight_specs = [_layer_spec(a, l) for a in weights]
    compiler_params = pltpu.CompilerParams(dimension_semantics=("arbitrary",), vmem_limit_bytes=VMEM_LIMIT_BYTES)
    if prefix is None:
        out = pl.pallas_call(
            _mlp_final_kernel,
            grid=(n // tt,),
            in_specs=[tile] + weight_specs + [_const_spec(params["final_g"].shape)],
            out_specs=tile,
            out_shape=jax.ShapeDtypeStruct((n, D_MODEL), F32),
            compiler_params=compiler_params,
            name="mlp_final",
        )(rows, *weights, params["final_g"])
        return out.reshape(shape)
    assert prefix.shape[0] % SUBLANES == 0
    out, prefix_out = pl.pallas_call(
        _mlp_with_prefix_kernel,
        grid=(n // tt,),
        in_specs=[tile, _const_spec(prefix.shape)] + weight_specs,
        out_specs=[tile, pl.BlockSpec(prefix.shape, lambda i: (0, 0))],
        out_shape=[jax.ShapeDtypeStruct((n, D_MODEL), F32), jax.ShapeDtypeStruct(prefix.shape, F32)],
        compiler_params=compiler_params,
        name="mlp",
    )(rows, prefix, *weights)
    return out.reshape(shape), prefix_out


def _block_diag(w, per_tile):
    L, n, d, _ = w.shape
    w = w.reshape(L, n // per_tile, per_tile, d, d)
    eye = jnp.eye(per_tile, dtype=w.dtype)
    bd = w[:, :, :, :, None, :] * eye[None, None, :, None, :, None]
    return bd.reshape(L, n // per_tile, per_tile * d, per_tile * d)


def kernel(x, meta_tokens, mix_norm_g, w_in, pool_w, pool_b, pool_scale, conv_w, conv_b, gate_r_w, gate_r_b,
           gate_i_w, gate_i_b, lru_lambda, group_norm_g, w_out, mlp_norm_g, w_up, w_down, final_norm_g):
    depth = w_in.shape[0]
    rows = lambda v: v.reshape(depth, 1, -1)
    gate_w = 0.5 * jnp.concatenate([_block_diag(gate_r_w, MXU_DIM // RNN_HEAD_DIM),
                                    _block_diag(gate_i_w, MXU_DIM // RNN_HEAD_DIM)], axis=-1)
    pool_tiles = (_block_diag(pool_w, MXU_DIM // POOL_GROUP_DIM)
                  * pool_scale.reshape(depth, D_POOL // MXU_DIM, 1, MXU_DIM))
    row_params = {"pool_b": pool_b, "pool_scale": pool_scale, "conv_b": conv_b, "gate_r_b": gate_r_b,
                  "gate_i_b": gate_i_b, "lam": lru_lambda}
    params = {
        "gains": jnp.stack([mix_norm_g, group_norm_g], axis=-1), "w_in": w_in,
        "pool_w": pool_tiles.astype(BF16), "rows": jnp.stack([row_params[k] for k in _ROW_PARAMS], axis=1),
        "conv_w": conv_w, "gate_w": gate_w.astype(BF16), "w_out": w_out, "mlp_g": rows(mlp_norm_g),
        "w_up": w_up, "w_down": w_down, "final_g": final_norm_g.reshape(1, -1),
    }

    hm = meta_tokens.astype(x.dtype)
    h = x
    for l in range(depth):
        hm, state = _prefix_mixer_call(hm[None], params, l)
        h = _mixer_call(h, state, params, l, tt=1024)
        if l + 1 < depth:
            h, hm = _mlp_call(h, params, l, tt=512, prefix=hm[0])
        else:
            h = _mlp_call(h, params, l, tt=512)
    return h
```
